```python
import jax, jax.numpy as jnp
from jax import lax
import numpy as np

D_MODEL = 2048
BATCH = 2
SEQ = 4096
DEPTH = 4

CHUNK = 64
EXPAND = 2
D_INNER = EXPAND * D_MODEL
HEAD_DIM = 128
N_HEADS = D_INNER // HEAD_DIM
N_LEFT_CHUNKS = 8
BAND = (N_LEFT_CHUNKS + 1) * CHUNK
REL_CLIP = 256
N_REL = 2 * REL_CLIP + 1
Q_BLOCK = 128
N_A_LAYERS = DEPTH // 2
N_B_LAYERS = DEPTH - N_A_LAYERS
EPS = 1e-6
NEG_INF = -1e30

kernel_name = "yoco_chunked_relbias_forgetting_attn"


def rms_norm(x, g):
    xf = x.astype(jnp.float32)
    y = xf * lax.rsqrt(jnp.mean(xf * xf, axis=-1, keepdims=True) + EPS)
    return (y * g.astype(jnp.float32)).astype(x.dtype)


def split_heads(t):
    return t.reshape(t.shape[0], t.shape[1], N_HEADS, HEAD_DIM)


def chunked_relbias_attention(q, k, v, rel_bias):
    b, s = q.shape[0], q.shape[1]
    n_chunks = s // CHUNK
    pad = N_LEFT_CHUNKS * CHUNK
    k_pad = jnp.pad(k, ((0, 0), (pad, 0), (0, 0), (0, 0)))
    v_pad = jnp.pad(v, ((0, 0), (pad, 0), (0, 0), (0, 0)))
    qi = jnp.arange(CHUNK)[:, None]
    kj = jnp.arange(BAND)[None, :]
    rel_idx = jnp.clip(pad + qi - kj, -REL_CLIP, REL_CLIP) + REL_CLIP
    bias = rel_bias.astype(jnp.float32)[:, rel_idx]
    scale = HEAD_DIM ** -0.5
    band_pos = jnp.arange(BAND)

    def one_chunk(c):
        start = c * CHUNK
        q_c = lax.dynamic_slice_in_dim(q, start, CHUNK, axis=1)
        k_c = lax.dynamic_slice_in_dim(k_pad, start, BAND, axis=1)
        v_c = lax.dynamic_slice_in_dim(v_pad, start, BAND, axis=1)
        sc = jnp.einsum('bqhd,bkhd->bhqk', q_c, k_c).astype(jnp.float32) * scale + bias
        valid = (start + band_pos) >= pad
        sc = jnp.where(valid[None, None, None, :], sc, NEG_INF)
        p = jax.nn.softmax(sc, axis=-1).astype(v.dtype)
        return jnp.einsum('bhqk,bkhd->bqhd', p, v_c)

    out = lax.map(one_chunk, jnp.arange(n_chunks))
    return jnp.moveaxis(out, 0, 1).reshape(b, s, N_HEADS, HEAD_DIM)


def forgetting_attention(q, k, v, cum_logf_h):
    s = q.shape[1]
    scale = HEAD_DIM ** -0.5
    outs = []
    for i in range(s // Q_BLOCK):
        q0 = i * Q_BLOCK
        end = q0 + Q_BLOCK
        q_b = q[:, q0:end]
        k_b = k[:, :end]
        v_b = v[:, :end]
        sc = jnp.einsum('bqhd,bkhd->bhqk', q_b, k_b).astype(jnp.float32) * scale
        decay = cum_logf_h[:, :, q0:end, None] - cum_logf_h[:, :, None, :end]
        causal = (q0 + jnp.arange(Q_BLOCK))[:, None] >= jnp.arange(end)[None, :]
        sc = jnp.where(causal[None, None], sc + decay, NEG_INF)
        p = jax.nn.softmax(sc, axis=-1).astype(v.dtype)
        outs.append(jnp.einsum('bhqk,bkhd->bqhd', p, v_b))
    return jnp.concatenate(outs, axis=1)


def setup_inputs(seed: int = 0) -> dict:
    key = jax.random.key(seed)
    ks = jax.random.split(key, 16)
    f32 = jnp.float32
    d_scale = D_MODEL ** -0.5
    out_scale = 0.5 * D_INNER ** -0.5
    x = jax.random.normal(ks[0], (BATCH, SEQ, D_MODEL), f32)
    a_norm = 1.0 + 0.02 * jax.random.normal(ks[1], (N_A_LAYERS, D_MODEL), f32)
    a_w_in = d_scale * jax.random.normal(ks[2], (N_A_LAYERS, D_MODEL, 4 * D_INNER), f32)
    a_rel_bias = 0.1 * jax.random.normal(ks[3], (N_A_LAYERS, N_HEADS, N_REL), f32)
    a_w_out = out_scale * jax.random.normal(ks[4], (N_A_LAYERS, D_INNER, D_MODEL), f32)
    kv_norm = 1.0 + 0.02 * jax.random.normal(ks[5], (D_MODEL,), f32)
    kv_w = d_scale * jax.random.normal(ks[6], (D_MODEL, 2 * D_INNER), f32)
    f_w = d_scale * jax.random.normal(ks[7], (D_MODEL, N_HEADS), f32)
    f_b = jnp.linspace(1.0, 5.0, N_HEADS, dtype=f32) + 0.1 * jax.random.normal(ks[8], (N_HEADS,), f32)
    b_norm = 1.0 + 0.02 * jax.random.normal(ks[9], (N_B_LAYERS, D_MODEL), f32)
    b_w_in = d_scale * jax.random.normal(ks[10], (N_B_LAYERS, D_MODEL, 2 * D_INNER), f32)
    b_w_out = out_scale * jax.random.normal(ks[11], (N_B_LAYERS, D_INNER, D_MODEL), f32)
    final_norm = 1.0 + 0.02 * jax.random.normal(ks[12], (D_MODEL,), f32)
    return {"x": x, "a_norm": a_norm, "a_w_in": a_w_in, "a_rel_bias": a_rel_bias,
            "a_w_out": a_w_out, "kv_norm": kv_norm, "kv_w": kv_w, "f_w": f_w,
            "f_b": f_b, "b_norm": b_norm, "b_w_in": b_w_in, "b_w_out": b_w_out,
            "final_norm": final_norm}


def reference(x, a_norm, a_w_in, a_rel_bias, a_w_out, kv_norm, kv_w, f_w, f_b,
              b_norm, b_w_in, b_w_out, final_norm):
    b, s, _ = x.shape
    k_sh = v_sh = cum_logf_h = None
    for layer in range(DEPTH):
        if layer < N_A_LAYERS:
            h = rms_norm(x, a_norm[layer])
            q, k, v, g = jnp.split(h @ a_w_in[layer], 4, axis=-1)
            o = chunked_relbias_attention(split_heads(q), split_heads(k), split_heads(v),
                                          a_rel_bias[layer]).reshape(b, s, D_INNER)
            x = x + (o * jax.nn.silu(g)) @ a_w_out[layer]
        else:
            if layer == N_A_LAYERS:
                h_kv = rms_norm(x, kv_norm)
                k_s, v_s = jnp.split(h_kv @ kv_w, 2, axis=-1)
                k_sh, v_sh = split_heads(k_s), split_heads(v_s)
                logf = jax.nn.log_sigmoid((h_kv @ f_w + f_b).astype(jnp.float32))
                cum_logf_h = jnp.transpose(jnp.cumsum(logf, axis=1), (0, 2, 1))
            lb = layer - N_A_LAYERS
            h = rms_norm(x, b_norm[lb])
            q, g = jnp.split(h @ b_w_in[lb], 2, axis=-1)
            o = forgetting_attention(split_heads(q), k_sh, v_sh, cum_logf_h).reshape(b, s, D_INNER)
            x = x + (o * jax.nn.silu(g)) @ b_w_out[lb]
    return rms_norm(x, final_norm)
```

```python
import functools

import jax
import jax.numpy as jnp
from jax import lax
from jax.experimental import pallas as pl
from jax.experimental.pallas import tpu as pltpu

D_MODEL = 2048
D_INNER = 4096
HEAD_DIM = 128
N_HEADS = D_INNER // HEAD_DIM
CHUNK = 64
N_LEFT_CHUNKS = 8
LEFT = N_LEFT_CHUNKS * CHUNK
BAND = LEFT + CHUNK
REL_CLIP = 256
EPS = 1e-6
NEG_INF = -1e30
LOG2E = 1.4426950408889634
QK_SCALE = HEAD_DIM ** -0.5 * LOG2E

LANE = 128
MIB = 1024 * 1024

F32 = jnp.float32
BF16 = jnp.bfloat16

A_TQ = 2 * CHUNK
A_WIN = LEFT + A_TQ
A_WBLK = A_WIN // LANE
A_TBLK = (LEFT + A_WIN) // LANE
A_ROW = A_TBLK * LANE + LANE


def _cparams(n_axes, vmem_mib):
    return pltpu.CompilerParams(dimension_semantics=("arbitrary",) * n_axes,
                                vmem_limit_bytes=vmem_mib * MIB)


def _rmsnorm_kernel(x_ref, g_ref, o_ref):
    x = x_ref[...]
    ms = jnp.mean(x * x, axis=-1, keepdims=True)
    o_ref[...] = (x * lax.rsqrt(ms + EPS) * g_ref[...]).astype(o_ref.dtype)


def _rmsnorm(x, gain, out_dtype, tm=512):
    m, d = x.shape
    return pl.pallas_call(
        _rmsnorm_kernel,
        grid=(m // tm,),
        in_specs=[pl.BlockSpec((tm, d), lambda i: (i, 0)),
                  pl.BlockSpec((1, d), lambda i: (0, 0))],
        out_specs=pl.BlockSpec((tm, d), lambda i: (i, 0)),
        out_shape=jax.ShapeDtypeStruct((m, d), out_dtype),
        compiler_params=_cparams(1, 32),
        name="rmsnorm",
    )(x, gain.reshape(1, d))


def _proj_kernel(h_ref, w_ref, o_ref, *, n_scaled_tiles, scale):
    j = pl.program_id(1)
    acc = jnp.dot(h_ref[...], w_ref[...].astype(BF16), preferred_element_type=F32)
    if n_scaled_tiles:
        acc = acc * jnp.where(j < n_scaled_tiles, scale, 1.0).astype(F32)
    for s in range(o_ref.shape[0]):
        o_ref[s] = acc[:, s * LANE:(s + 1) * LANE].astype(o_ref.dtype)


def _proj(h, w, n_scaled_cols, tm=2048, tn=512):
    m, k = h.shape
    n = w.shape[1]
    kern = functools.partial(_proj_kernel, n_scaled_tiles=n_scaled_cols // tn, scale=QK_SCALE)
    return pl.pallas_call(
        kern,
        grid=(m // tm, n // tn),
        in_specs=[pl.BlockSpec((tm, k), lambda i, j: (i, 0)),
                  pl.BlockSpec((k, tn), lambda i, j: (0, j))],
        out_specs=pl.BlockSpec((tn // LANE, tm, LANE), lambda i, j: (j, i, 0)),
        out_shape=jax.ShapeDtypeStruct((n // LANE, m, LANE), BF16),
        compiler_params=_cparams(2, 48),
        name="proj",
    )(h, w)


def _bias_kernel(w_ref, o_ref):
    x = jnp.broadcast_to(w_ref[0], (A_TQ, A_ROW))
    r = pltpu.roll(x, A_ROW - A_TQ, 1, stride=1, stride_axis=0)
    width = A_TBLK * LANE
    i = lax.broadcasted_iota(jnp.int32, (A_TQ, width), 0)
    u = lax.broadcasted_iota(jnp.int32, (A_TQ, width), 1)
    lo = (i // CHUNK) * CHUNK
    valid = (u >= lo) & (u < lo + BAND)
    t = jnp.where(valid, r[:, :width] * LOG2E, NEG_INF)
    for b in range(A_TBLK):
        o_ref[0, b] = t[:, b * LANE:(b + 1) * LANE]


def _bias_table(rel_bias):
    h = rel_bias.shape[0]
    n_hi = LEFT + A_TQ - REL_CLIP
    n_lo = A_ROW - n_hi - (2 * REL_CLIP + 1)
    row = jnp.concatenate([jnp.broadcast_to(rel_bias[:, -1:], (h, n_hi)),
                           rel_bias[:, ::-1],
                           jnp.broadcast_to(rel_bias[:, :1], (h, n_lo))], axis=1)
    return pl.pallas_call(
        _bias_kernel,
        grid=(h,),
        in_specs=[pl.BlockSpec((1, 1, A_ROW), lambda i: (i, 0, 0))],
        out_specs=pl.BlockSpec((1, A_TBLK, A_TQ, LANE), lambda i: (i, 0, 0, 0)),
        out_shape=jax.ShapeDtypeStruct((h, A_TBLK, A_TQ, LANE), F32),
        compiler_params=_cparams(1, 32),
        name="bias_table",
    )(row.reshape(h, 1, A_ROW))


def _attn_a_kernel(q_ref, k_ref, v_ref, g_ref, tt_ref, o_ref, *, hb, seq):
    def tile(t, carry):
        q0 = pl.multiple_of(t * A_TQ, A_TQ)
        ws = pl.multiple_of(jnp.maximum(q0 - LEFT, 0), LANE)
        ob = LEFT // LANE - jnp.minimum(t, LEFT // LANE)
        for hh in range(hb):
            q = q_ref[hh, pl.ds(q0, A_TQ), :]
            kw = k_ref[hh, pl.ds(ws, A_WIN), :]
            vw = v_ref[hh, pl.ds(ws, A_WIN), :]
            s = lax.dot_general(q, kw, (((1,), (1,)), ((), ())), preferred_element_type=F32)
            s = jnp.concatenate([s[:, b * LANE:(b + 1) * LANE] + tt_ref[hh, ob + b]
                                 for b in range(A_WBLK)], axis=1)
            m = jnp.max(s, axis=1, keepdims=True)
            p = jnp.exp2(s - m)
            l = jnp.sum(p, axis=1, keepdims=True)
            o = jnp.dot(p.astype(BF16), vw, preferred_element_type=F32) / l
            g = g_ref[hh, pl.ds(q0, A_TQ), :].astype(F32)
            o_ref[pl.ds(q0, A_TQ), hh * LANE:(hh + 1) * LANE] = (o * g * jax.nn.sigmoid(g)).astype(o_ref.dtype)
        return carry

    lax.fori_loop(0, seq // A_TQ, tile, 0)


def _attn_a(qkvg, tt, batch, seq, hb=2):
    nblk = N_HEADS // hb
    kern = functools.partial(_attn_a_kernel, hb=hb, seq=seq)

    def slab(role):
        return pl.BlockSpec((hb, seq, LANE), lambda b, h: (role * nblk + h, b, 0))

    return pl.pallas_call(
        kern,
        grid=(batch, nblk),
        in_specs=[slab(0), slab(1), slab(2), slab(3),
                  pl.BlockSpec((hb, A_TBLK, A_TQ, LANE), lambda b, h: (h, 0, 0, 0))],
        out_specs=pl.BlockSpec((seq, hb * LANE), lambda b, h: (b, h)),
        out_shape=jax.ShapeDtypeStruct((batch * seq, D_INNER), BF16),
        compiler_params=_cparams(2, 48),
        name="attn_a",
    )(qkvg, qkvg, qkvg, qkvg, tt)


def _outproj_kernel(a_ref, w_ref, x_ref, o_ref):
    o_ref[...] = x_ref[...] + jnp.dot(a_ref[...], w_ref[...].astype(BF16), preferred_element_type=F32)


def _outproj(a, w, x, tm=1024, tn=512):
    m, k = a.shape
    n = w.shape[1]
    return pl.pallas_call(
        _outproj_kernel,
        grid=(m // tm, n // tn),
        in_specs=[pl.BlockSpec((tm, k), lambda i, j: (i, 0)),
                  pl.BlockSpec((k, tn), lambda i, j: (0, j)),
                  pl.BlockSpec((tm, tn), lambda i, j: (i, j))],
        out_specs=pl.BlockSpec((tm, tn), lambda i, j: (i, j)),
        out_shape=jax.ShapeDtypeStruct((m, n), F32),
        compiler_params=_cparams(2, 56),
        name="outproj",
    )(a, w, x)


def _split3(x):
    hi = x.astype(BF16)
    r1 = x - hi.astype(F32)
    mid = r1.astype(BF16)
    lo = (r1 - mid.astype(F32)).astype(BF16)
    return hi, mid, lo


def _fgate_kernel(h_ref, fw_ref, fb_ref, qx_ref, kx_ref, carry_ref, *, ts):
    @pl.when(pl.program_id(1) == 0)
    def _():
        carry_ref[...] = jnp.zeros_like(carry_ref)

    z = jnp.dot(h_ref[...], fw_ref[...].astype(BF16), preferred_element_type=F32) + fb_ref[...]
    logf = jnp.minimum(z, 0.0) - jnp.log1p(jnp.exp(-jnp.abs(z)))
    r = lax.broadcasted_iota(jnp.int32, (ts, ts), 0)
    c = lax.broadcasted_iota(jnp.int32, (ts, ts), 1)
    tri = jnp.where(r >= c, 1.0, 0.0).astype(BF16)
    cum = carry_ref[...]
    for part in _split3(logf):
        cum = cum + jnp.dot(tri, part, preferred_element_type=F32)
    carry_ref[...] = cum[ts - 1:ts, :]
    parts = jnp.concatenate(_split3(cum * LOG2E), axis=1)
    rr = lax.broadcasted_iota(jnp.int32, (3 * LANE, 2 * LANE), 0)
    cc = lax.broadcasted_iota(jnp.int32, (3 * LANE, 2 * LANE), 1)
    lane = lax.broadcasted_iota(jnp.int32, (1, 2 * LANE), 1)
    ones = jnp.where((lane >= 3) & (lane < 6) | (lane >= LANE) & (lane < LANE + 3), 1.0, 0.0)
    for hd in range(N_HEADS):
        sel = (jnp.where((cc < 3) & (rr == cc * LANE + hd), 1.0, 0.0)
               - jnp.where((cc >= LANE + 3) & (cc < LANE + 6) & (rr == (cc - LANE - 3) * LANE + hd), 1.0, 0.0))
        ex = jnp.dot(parts, sel.astype(BF16), preferred_element_type=F32) + ones
        qx_ref[hd] = ex[:, :LANE].astype(qx_ref.dtype)
        kx_ref[hd] = ex[:, LANE:].astype(kx_ref.dtype)


def _fgate(h, f_w, f_b, batch, seq, ts=512):
    m, d = h.shape
    fw = jnp.pad(f_w, ((0, 0), (0, LANE - N_HEADS)))
    fb = jnp.pad(f_b, (0, LANE - N_HEADS)).reshape(1, LANE)
    nblk = seq // ts
    ex_spec = pl.BlockSpec((N_HEADS, ts, LANE), lambda b, s: (0, b * nblk + s, 0))
    ex_shape = jax.ShapeDtypeStruct((N_HEADS, m, LANE), BF16)
    return pl.pallas_call(
        functools.partial(_fgate_kernel, ts=ts),
        grid=(batch, nblk),
        in_specs=[pl.BlockSpec((ts, d), lambda b, s: (b * nblk + s, 0)),
                  pl.BlockSpec((d, LANE), lambda b, s: (0, 0)),
                  pl.BlockSpec((1, LANE), lambda b, s: (0, 0))],
        out_specs=(ex_spec, ex_spec),
        out_shape=(ex_shape, ex_shape),
        scratch_shapes=[pltpu.VMEM((1, LANE), F32)],
        compiler_params=_cparams(2, 48),
        name="fgate",
    )(h, fw, fb)


def _attn_b_kernel(q_ref, qx_ref, k_ref, kx_ref, v_ref, g_ref, o_ref, *, hb, seq, tq, tk):
    sub = tq // tk
    row = lax.broadcasted_iota(jnp.int32, (tq, tk), 0)
    col = lax.broadcasted_iota(jnp.int32, (tq, tk), 1)

    for hh in range(hb):
        def kv_step(qa, k0, m, l, acc, diag_off, hh=hh):
            ka = jnp.concatenate([k_ref[hh, pl.ds(k0, tk), :], kx_ref[hh, pl.ds(k0, tk), :]], axis=1)
            s = lax.dot_general(qa, ka, (((1,), (1,)), ((), ())), preferred_element_type=F32)
            if diag_off is not None:
                s = jnp.where(row >= col + diag_off, s, NEG_INF)
            m_new = jnp.maximum(m, jnp.max(s, axis=1, keepdims=True))
            alpha = jnp.exp2(m - m_new)
            p = jnp.exp2(s - m_new)
            l = alpha * l + jnp.sum(p, axis=1, keepdims=True)
            acc = alpha * acc + jnp.dot(p.astype(BF16), v_ref[hh, pl.ds(k0, tk), :],
                                        preferred_element_type=F32)
            return m_new, l, acc

        def q_tile(i, carry, hh=hh, kv_step=kv_step):
            q0 = pl.multiple_of(i * tq, tq)
            qa = jnp.concatenate([q_ref[hh, pl.ds(q0, tq), :], qx_ref[hh, pl.ds(q0, tq), :]], axis=1)
            init = (jnp.full((tq, 1), NEG_INF, F32), jnp.zeros((tq, 1), F32), jnp.zeros((tq, HEAD_DIM), F32))
            m, l, acc = lax.fori_loop(
                0, i * sub, lambda j, c: kv_step(qa, pl.multiple_of(j * tk, tk), *c, None), init)
            for d in range(sub):
                m, l, acc = kv_step(qa, pl.multiple_of(q0 + d * tk, tk), m, l, acc, d * tk)
            g = g_ref[hh, pl.ds(q0, tq), :].astype(F32)
            o_ref[pl.ds(q0, tq), hh * LANE:(hh + 1) * LANE] = (acc / l * g * jax.nn.sigmoid(g)).astype(o_ref.dtype)
            return carry

        lax.fori_loop(0, seq // tq, q_tile, 0)


def _attn_b(qg, kv, qx, kx, batch, seq, hb=2, tq=256, tk=256):
    nblk = N_HEADS // hb
    kern = functools.partial(_attn_b_kernel, hb=hb, seq=seq, tq=tq, tk=tk)

    def slab(role):
        return pl.BlockSpec((hb, seq, LANE), lambda b, h: (role * nblk + h, b, 0))

    return pl.pallas_call(
        kern,
        grid=(batch, nblk),
        in_specs=[slab(0), slab(0), slab(0), slab(0), slab(1), slab(1)],
        out_specs=pl.BlockSpec((seq, hb * LANE), lambda b, h: (b, h)),
        out_shape=jax.ShapeDtypeStruct((batch * seq, D_INNER), BF16),
        compiler_params=_cparams(2, 48),
        name="attn_b",
    )(qg, qx, kv, kx, kv, qg)


def kernel(x, a_norm, a_w_in, a_rel_bias, a_w_out, kv_norm, kv_w, f_w, f_b, b_norm, b_w_in, b_w_out, final_norm):
    batch, seq, d = x.shape
    xr = x.reshape(batch * seq, d)
    for layer in range(a_w_in.shape[0]):
        h = _rmsnorm(xr, a_norm[layer], BF16)
        qkvg = _proj(h, a_w_in[layer], D_INNER)
        tt = _bias_table(a_rel_bias[layer])
        a = _attn_a(qkvg, tt, batch, seq)
        xr = _outproj(a, a_w_out[layer], xr)
    h_kv = _rmsnorm(xr, kv_norm, BF16)
    kv = _proj(h_kv, kv_w, 0)
    qx, kx = _fgate(h_kv, f_w, f_b, batch, seq)
    for layer in range(b_w_in.shape[0]):
        h = _rmsnorm(xr, b_norm[layer], BF16)
        qg = _proj(h, b_w_in[layer], D_INNER)
        a = _attn_b(qg, kv, qx, kx, batch, seq)
        xr = _outproj(a, b_w_out[layer], xr)
    return _rmsnorm(xr, final_norm, x.dtype).reshape(batch, seq, d)
```

```python
import functools

import jax
import jax.numpy as jnp
from jax import lax
from jax.experimental import pallas as pl
from jax.experimental.pallas import tpu as pltpu

D_MODEL = 2048
D_INNER = 4096
HEAD_DIM = 128
N_HEADS = D_INNER // HEAD_DIM
CHUNK = 64
N_LEFT_CHUNKS = 8
LEFT = N_LEFT_CHUNKS * CHUNK
BAND = LEFT + CHUNK
REL_CLIP = 256
EPS = 1e-6
NEG_INF = -1e30
LOG2E = 1.4426950408889634
QK_SCALE = HEAD_DIM ** -0.5 * LOG2E

LANE = 128
MIB = 1024 * 1024

F32 = jnp.float32
BF16 = jnp.bfloat16

A_TQ = 4 * CHUNK
A_WIN = LEFT + A_TQ
A_TU = LEFT + A_WIN
A_ROW = A_TU + A_TQ


def _cparams(n_axes, vmem_mib):
    return pltpu.CompilerParams(dimension_semantics=("arbitrary",) * n_axes,
                                vmem_limit_bytes=vmem_mib * MIB)


def _rmsnorm_kernel(x_ref, g_ref, o_ref):
    x = x_ref[...]
    ms = jnp.mean(x * x, axis=-1, keepdims=True)
    o_ref[...] = (x * lax.rsqrt(ms + EPS) * g_ref[...]).astype(o_ref.dtype)


def _rmsnorm(x, gain, out_dtype, tm=512):
    m, d = x.shape
    return pl.pallas_call(
        _rmsnorm_kernel,
        grid=(m // tm,),
        in_specs=[pl.BlockSpec((tm, d), lambda i: (i, 0)),
                  pl.BlockSpec((1, d), lambda i: (0, 0))],
        out_specs=pl.BlockSpec((tm, d), lambda i: (i, 0)),
        out_shape=jax.ShapeDtypeStruct((m, d), out_dtype),
        compiler_params=_cparams(1, 32),
        name="rmsnorm",
    )(x, gain.reshape(1, d))


def _proj_kernel(h_ref, w_ref, o_ref, *, n_scaled_tiles, scale):
    j = pl.program_id(1)
    acc = jnp.dot(h_ref[...], w_ref[...].astype(BF16), preferred_element_type=F32)
    if n_scaled_tiles:
        acc = acc * jnp.where(j < n_scaled_tiles, scale, 1.0).astype(F32)
    for s in range(o_ref.shape[0]):
        o_ref[s] = acc[:, s * LANE:(s + 1) * LANE].astype(o_ref.dtype)


def _proj(h, w, layer, n_scaled_cols, tm=2048, tn=512):
    m, k = h.shape
    n = w.shape[2]
    kern = functools.partial(_proj_kernel, n_scaled_tiles=n_scaled_cols // tn, scale=QK_SCALE)
    return pl.pallas_call(
        kern,
        grid=(m // tm, n // tn),
        in_specs=[pl.BlockSpec((tm, k), lambda i, j: (i, 0)),
                  pl.BlockSpec((None, k, tn), lambda i, j: (layer, 0, j))],
        out_specs=pl.BlockSpec((tn // LANE, tm, LANE), lambda i, j: (j, i, 0)),
        out_shape=jax.ShapeDtypeStruct((n // LANE, m, LANE), BF16),
        compiler_params=_cparams(2, 48),
        name="proj",
    )(h, w)


def _emit_skewed(n_tasks, stages):
    k = len(stages)
    order = [0] + list(range(k - 1, 0, -1))
    vals, out = {}, [None] * n_tasks
    for step in range(n_tasks + k - 1):
        for si in order:
            t = step - si
            if 0 <= t < n_tasks:
                res = stages[si](t, vals.pop((t, si - 1)) if si else None)
                if si == k - 1:
                    out[t] = res
                else:
                    vals[(t, si)] = res
    return out


def _bias_kernel(w_ref, o_ref):
    x = jnp.broadcast_to(w_ref[0], (A_TQ, A_ROW))
    r = pltpu.roll(x, A_ROW - A_TQ, 1, stride=1, stride_axis=0)
    i = lax.broadcasted_iota(jnp.int32, (A_TQ, A_TU), 0)
    u = lax.broadcasted_iota(jnp.int32, (A_TQ, A_TU), 1)
    lo = (i // CHUNK) * CHUNK
    valid = (u >= lo) & (u < lo + BAND)
    o_ref[0] = jnp.where(valid, r[:, :A_TU] * LOG2E, NEG_INF).T


def _bias_table(rel_bias):
    h = rel_bias.shape[0]
    n_hi = LEFT + A_TQ - REL_CLIP
    n_lo = A_ROW - n_hi - (2 * REL_CLIP + 1)
    row = jnp.concatenate([jnp.broadcast_to(rel_bias[:, -1:], (h, n_hi)),
                           rel_bias[:, ::-1],
                           jnp.broadcast_to(rel_bias[:, :1], (h, n_lo))], axis=1)
    return pl.pallas_call(
        _bias_kernel,
        grid=(h,),
        in_specs=[pl.BlockSpec((1, 1, A_ROW), lambda i: (i, 0, 0))],
        out_specs=pl.BlockSpec((1, A_TU, A_TQ), lambda i: (i, 0, 0)),
        out_shape=jax.ShapeDtypeStruct((h, A_TU, A_TQ), F32),
        compiler_params=_cparams(1, 32),
        name="bias_table",
    )(row.reshape(h, 1, A_ROW))


def _attn_a_kernel(q_ref, k_ref, v_ref, g_ref, tt_ref, o_ref, *, hb, nt):
    blk = pl.program_id(2)
    tasks = [(hh, t) for t in range(nt) for hh in range(hb)]

    def window(t):
        q0 = (blk * nt + t) * A_TQ
        ws = pl.multiple_of(jnp.maximum(q0 - LEFT, 0), A_TQ)
        u0 = pl.multiple_of(LEFT - jnp.minimum(q0, LEFT), A_TQ)
        return ws, u0

    def scores(i, _):
        hh, t = tasks[i]
        ws, u0 = window(t)
        s = lax.dot_general(k_ref[hh, pl.ds(ws, A_WIN), :], q_ref[hh, t * A_TQ:(t + 1) * A_TQ, :],
                            (((1,), (1,)), ((), ())), preferred_element_type=F32)
        return s + tt_ref[hh, pl.ds(u0, A_WIN), :]

    def softmax(i, s):
        p = jnp.exp2(s - jnp.max(s, axis=0, keepdims=True))
        return p.astype(BF16), jnp.sum(p, axis=0, keepdims=True)

    def output(i, p_l):
        p, l = p_l
        hh, t = tasks[i]
        ws, _ = window(t)
        ot = lax.dot_general(v_ref[hh, pl.ds(ws, A_WIN), :], p, (((0,), (0,)), ((), ())),
                             preferred_element_type=F32)
        g = g_ref[hh, t * A_TQ:(t + 1) * A_TQ, :].astype(F32)
        o_ref[t * A_TQ:(t + 1) * A_TQ, hh * LANE:(hh + 1) * LANE] = (
            (ot / l).T * g * jax.nn.sigmoid(g)).astype(o_ref.dtype)

    _emit_skewed(len(tasks), [scores, softmax, output])


def _attn_a(qkvg, tt, batch, seq, hb=4, nt=2):
    nblk = N_HEADS // hb
    tqb = nt * A_TQ
    nq = seq // tqb
    kern = functools.partial(_attn_a_kernel, hb=hb, nt=nt)

    def slab(role):
        return pl.BlockSpec((hb, seq, LANE), lambda b, h, i: (role * nblk + h, b, 0))

    def tile(role):
        return pl.BlockSpec((hb, tqb, LANE), lambda b, h, i: (role * nblk + h, b * nq + i, 0))

    return pl.pallas_call(
        kern,
        grid=(batch, nblk, nq),
        in_specs=[tile(0), slab(1), slab(2), tile(3),
                  pl.BlockSpec((hb, A_TU, A_TQ), lambda b, h, i: (h, 0, 0))],
        out_specs=pl.BlockSpec((tqb, hb * LANE), lambda b, h, i: (b * nq + i, h)),
        out_shape=jax.ShapeDtypeStruct((batch * seq, D_INNER), BF16),
        compiler_params=_cparams(3, 56),
        name="attn_a",
    )(qkvg, qkvg, qkvg, qkvg, tt)


def _outproj_kernel(a_ref, w_ref, x_ref, o_ref):
    o_ref[...] = x_ref[...] + jnp.dot(a_ref[...], w_ref[...].astype(BF16), preferred_element_type=F32)


def _outproj(a, w, layer, x, tm=1024, tn=512):
    m, k = a.shape
    n = w.shape[2]
    return pl.pallas_call(
        _outproj_kernel,
        grid=(m // tm, n // tn),
        in_specs=[pl.BlockSpec((tm, k), lambda i, j: (i, 0)),
                  pl.BlockSpec((None, k, tn), lambda i, j: (layer, 0, j)),
                  pl.BlockSpec((tm, tn), lambda i, j: (i, j))],
        out_specs=pl.BlockSpec((tm, tn), lambda i, j: (i, j)),
        out_shape=jax.ShapeDtypeStruct((m, n), F32),
        compiler_params=_cparams(2, 56),
        name="outproj",
    )(a, w, x)


def _split3(x):
    hi = x.astype(BF16)
    r1 = x - hi.astype(F32)
    mid = r1.astype(BF16)
    lo = (r1 - mid.astype(F32)).astype(BF16)
    return hi, mid, lo


def _fgate_kernel(h_ref, fw_ref, fb_ref, qx_ref, kx_ref, carry_ref, *, ts):
    @pl.when(pl.program_id(1) == 0)
    def _():
        carry_ref[...] = jnp.zeros_like(carry_ref)

    z = jnp.dot(h_ref[...], fw_ref[...].astype(BF16), preferred_element_type=F32) + fb_ref[...]
    logf = jnp.minimum(z, 0.0) - jnp.log1p(jnp.exp(-jnp.abs(z)))
    r = lax.broadcasted_iota(jnp.int32, (ts, ts), 0)
    c = lax.broadcasted_iota(jnp.int32, (ts, ts), 1)
    tri = jnp.where(r >= c, 1.0, 0.0).astype(BF16)
    cum = carry_ref[...]
    for part in _split3(logf):
        cum = cum + jnp.dot(tri, part, preferred_element_type=F32)
    carry_ref[...] = cum[ts - 1:ts, :]
    parts = jnp.concatenate(_split3(cum * LOG2E), axis=1)
    rr = lax.broadcasted_iota(jnp.int32, (3 * LANE, 2 * LANE), 0)
    cc = lax.broadcasted_iota(jnp.int32, (3 * LANE, 2 * LANE), 1)
    lane = lax.broadcasted_iota(jnp.int32, (1, 2 * LANE), 1)
    ones = jnp.where((lane >= 3) & (lane < 6) | (lane >= LANE) & (lane < LANE + 3), 1.0, 0.0)
    for hd in range(N_HEADS):
        sel = (jnp.where((cc < 3) & (rr == cc * LANE + hd), 1.0, 0.0)
               - jnp.where((cc >= LANE + 3) & (cc < LANE + 6) & (rr == (cc - LANE - 3) * LANE + hd), 1.0, 0.0))
        ex = jnp.dot(parts, sel.astype(BF16), preferred_element_type=F32) + ones
        qx_ref[hd] = ex[:, :LANE].astype(qx_ref.dtype)
        kx_ref[hd] = ex[:, LANE:].astype(kx_ref.dtype)


def _fgate(h, f_w, f_b, batch, seq, ts=512):
    m, d = h.shape
    fw = jnp.pad(f_w, ((0, 0), (0, LANE - N_HEADS)))
    fb = jnp.pad(f_b, (0, LANE - N_HEADS)).reshape(1, LANE)
    nblk = seq // ts
    ex_spec = pl.BlockSpec((N_HEADS, ts, LANE), lambda b, s: (0, b * nblk + s, 0))
    ex_shape = jax.ShapeDtypeStruct((N_HEADS, m, LANE), BF16)
    return pl.pallas_call(
        functools.partial(_fgate_kernel, ts=ts),
        grid=(batch, nblk),
        in_specs=[pl.BlockSpec((ts, d), lambda b, s: (b * nblk + s, 0)),
                  pl.BlockSpec((d, LANE), lambda b, s: (0, 0)),
                  pl.BlockSpec((1, LANE), lambda b, s: (0, 0))],
        out_specs=(ex_spec, ex_spec),
        out_shape=(ex_shape, ex_shape),
        scratch_shapes=[pltpu.VMEM((1, LANE), F32)],
        compiler_params=_cparams(2, 48),
        name="fgate",
    )(h, fw, fb)


def _attn_b_kernel(q_ref, qx_ref, k_ref, kx_ref, v_ref, g_ref, o_ref, *, hb, tq, tk):
    sub = tq // tk
    i = pl.program_id(2)
    q0 = i * tq
    row = lax.broadcasted_iota(jnp.int32, (tk, tq), 0)
    col = lax.broadcasted_iota(jnp.int32, (tk, tq), 1)
    qa = [jnp.concatenate([q_ref[hh], qx_ref[hh]], axis=1) for hh in range(hb)]

    def kv_step(k0, state, diag_off):
        def scores(hh, _):
            ka = jnp.concatenate([k_ref[hh, pl.ds(k0, tk), :], kx_ref[hh, pl.ds(k0, tk), :]], axis=1)
            s = lax.dot_general(ka, qa[hh], (((1,), (1,)), ((), ())), preferred_element_type=F32)
            if diag_off is not None:
                s = jnp.where(row + diag_off <= col, s, NEG_INF)
            return s

        def softmax(hh, s):
            m, l, _ = state[hh]
            m_new = jnp.maximum(m, jnp.max(s, axis=0, keepdims=True))
            alpha = jnp.exp2(m - m_new)
            p = jnp.exp2(s - m_new)
            return m_new, alpha * l + jnp.sum(p, axis=0, keepdims=True), alpha, p.astype(BF16)

        def accumulate(hh, sm):
            m_new, l_new, alpha, p = sm
            pv = lax.dot_general(v_ref[hh, pl.ds(k0, tk), :], p, (((0,), (0,)), ((), ())),
                                 preferred_element_type=F32)
            return m_new, l_new, alpha * state[hh][2] + pv

        return tuple(_emit_skewed(hb, [scores, softmax, accumulate]))

    init = tuple((jnp.full((1, tq), NEG_INF, F32), jnp.zeros((1, tq), F32), jnp.zeros((HEAD_DIM, tq), F32))
                 for _ in range(hb))
    state = lax.fori_loop(0, i * sub, lambda j, c: kv_step(pl.multiple_of(j * tk, tk), c, None), init)
    for d in range(sub):
        state = kv_step(pl.multiple_of(q0 + d * tk, tk), state, d * tk)
    for hh in range(hb):
        _, l, acc = state[hh]
        g = g_ref[hh].astype(F32)
        o_ref[:, hh * LANE:(hh + 1) * LANE] = ((acc / l).T * g * jax.nn.sigmoid(g)).astype(o_ref.dtype)


def _attn_b(qg, kv, qx, kx, batch, seq, hb=4, tq=512, tk=512):
    nblk = N_HEADS // hb
    nq = seq // tq
    kern = functools.partial(_attn_b_kernel, hb=hb, tq=tq, tk=tk)

    def slab(role):
        return pl.BlockSpec((hb, seq, LANE), lambda b, h, i: (role * nblk + h, b, 0))

    def tile(role):
        return pl.BlockSpec((hb, tq, LANE), lambda b, h, i: (role * nblk + h, b * nq + i, 0))

    return pl.pallas_call(
        kern,
        grid=(batch, nblk, nq),
        in_specs=[tile(0), tile(0), slab(0), slab(0), slab(1), tile(1)],
        out_specs=pl.BlockSpec((tq, hb * LANE), lambda b, h, i: (b * nq + i, h)),
        out_shape=jax.ShapeDtypeStruct((batch * seq, D_INNER), BF16),
        compiler_params=_cparams(3, 48),
        name="attn_b",
    )(qg, qx, kv, kx, kv, qg)


def kernel(x, a_norm, a_w_in, a_rel_bias, a_w_out, kv_norm, kv_w, f_w, f_b, b_norm, b_w_in, b_w_out, final_norm):
    batch, seq, d = x.shape
    xr = x.reshape(batch * seq, d)
    for layer in range(a_w_in.shape[0]):
        h = _rmsnorm(xr, a_norm[layer], BF16)
        qkvg = _proj(h, a_w_in, layer, D_INNER)
        tt = _bias_table(a_rel_bias[layer])
        a = _attn_a(qkvg, tt, batch, seq)
        xr = _outproj(a, a_w_out, layer, xr)
    h_kv = _rmsnorm(xr, kv_norm, BF16)
    kv = _proj(h_kv, kv_w[None], 0, 0)
    qx, kx = _fgate(h_kv, f_w, f_b, batch, seq)
    for layer in range(b_w_in.shape[0]):
        h = _rmsnorm(xr, b_norm[layer], BF16)
        qg = _proj(h, b_w_in, layer, D_INNER)
        a = _attn_b(qg, kv, qx, kx, batch, seq)
        xr = _outproj(a, b_w_out, layer, xr)
    return _rmsnorm(xr, final_norm, x.dtype).reshape(batch, seq, d)
```

```python
import functools

import jax
import jax.numpy as jnp
from jax import lax
from jax.experimental import pallas as pl
from jax.experimental.pallas import tpu as pltpu

D_MODEL = 2048
D_INNER = 4096
HEAD_DIM = 128
N_HEADS = D_INNER // HEAD_DIM
CHUNK = 64
N_LEFT_CHUNKS = 8
LEFT = N_LEFT_CHUNKS * CHUNK
BAND = LEFT + CHUNK
REL_CLIP = 256
EPS = 1e-6
NEG_INF = -1e30
LOG2E = 1.4426950408889634
QK_SCALE = HEAD_DIM ** -0.5 * LOG2E

LANE = 128
MIB = 1024 * 1024

F32 = jnp.float32
BF16 = jnp.bfloat16

A_TQ = 4 * CHUNK
A_WIN = LEFT + A_TQ
A_TU = LEFT + A_WIN
A_ROW = A_TU + A_TQ


def _cparams(n_axes, vmem_mib):
    return pltpu.CompilerParams(dimension_semantics=("arbitrary",) * n_axes,
                                vmem_limit_bytes=vmem_mib * MIB)


def _rmsnorm_kernel(x_ref, g_ref, o_ref):
    x = x_ref[...]
    ms = jnp.mean(x * x, axis=-1, keepdims=True)
    o_ref[...] = (x * lax.rsqrt(ms + EPS) * g_ref[...]).astype(o_ref.dtype)


def _rmsnorm(x, gain, out_dtype, tm=512):
    m, d = x.shape
    return pl.pallas_call(
        _rmsnorm_kernel,
        grid=(m // tm,),
        in_specs=[pl.BlockSpec((tm, d), lambda i: (i, 0)),
                  pl.BlockSpec((1, d), lambda i: (0, 0))],
        out_specs=pl.BlockSpec((tm, d), lambda i: (i, 0)),
        out_shape=jax.ShapeDtypeStruct((m, d), out_dtype),
        compiler_params=_cparams(1, 32),
        name="rmsnorm",
    )(x, gain.reshape(1, d))


def _proj_kernel(h_ref, w_ref, o_ref, *, n_scaled_tiles, scale):
    j = pl.program_id(1)
    acc = jnp.dot(h_ref[...], w_ref[...].astype(BF16), preferred_element_type=F32)
    if n_scaled_tiles:
        acc = acc * jnp.where(j < n_scaled_tiles, scale, 1.0).astype(F32)
    for s in range(o_ref.shape[0]):
        o_ref[s] = acc[:, s * LANE:(s + 1) * LANE].astype(o_ref.dtype)


def _proj(h, w, layer, n_scaled_cols, tm=2048, tn=512):
    m, k = h.shape
    n = w.shape[2]
    kern = functools.partial(_proj_kernel, n_scaled_tiles=n_scaled_cols // tn, scale=QK_SCALE)
    return pl.pallas_call(
        kern,
        grid=(m // tm, n // tn),
        in_specs=[pl.BlockSpec((tm, k), lambda i, j: (i, 0)),
                  pl.BlockSpec((None, k, tn), lambda i, j: (layer, 0, j))],
        out_specs=pl.BlockSpec((tn // LANE, tm, LANE), lambda i, j: (j, i, 0)),
        out_shape=jax.ShapeDtypeStruct((n // LANE, m, LANE), BF16),
        compiler_params=_cparams(2, 48),
        name="proj",
    )(h, w)


def _emit_skewed(n_tasks, stages):
    k = len(stages)
    order = [0] + list(range(k - 1, 0, -1))
    vals, out = {}, [None] * n_tasks
    for step in range(n_tasks + k - 1):
        for si in order:
            t = step - si
            if 0 <= t < n_tasks:
                res = stages[si](t, vals.pop((t, si - 1)) if si else None)
                if si == k - 1:
                    out[t] = res
                else:
                    vals[(t, si)] = res
    return out


def _bias_kernel(w_ref, o_ref):
    x = jnp.broadcast_to(w_ref[0], (A_TQ, A_ROW))
    r = pltpu.roll(x, A_ROW - A_TQ, 1, stride=1, stride_axis=0)
    i = lax.broadcasted_iota(jnp.int32, (A_TQ, A_TU), 0)
    u = lax.broadcasted_iota(jnp.int32, (A_TQ, A_TU), 1)
    lo = (i // CHUNK) * CHUNK
    valid = (u >= lo) & (u < lo + BAND)
    o_ref[0] = jnp.where(valid, r[:, :A_TU] * LOG2E, NEG_INF).T


def _bias_table(rel_bias):
    h = rel_bias.shape[0]
    n_hi = LEFT + A_TQ - REL_CLIP
    n_lo = A_ROW - n_hi - (2 * REL_CLIP + 1)
    row = jnp.concatenate([jnp.broadcast_to(rel_bias[:, -1:], (h, n_hi)),
                           rel_bias[:, ::-1],
                           jnp.broadcast_to(rel_bias[:, :1], (h, n_lo))], axis=1)
    return pl.pallas_call(
        _bias_kernel,
        grid=(h,),
        in_specs=[pl.BlockSpec((1, 1, A_ROW), lambda i: (i, 0, 0))],
        out_specs=pl.BlockSpec((1, A_TU, A_TQ), lambda i: (i, 0, 0)),
        out_shape=jax.ShapeDtypeStruct((h, A_TU, A_TQ), F32),
        compiler_params=_cparams(1, 32),
        name="bias_table",
    )(row.reshape(h, 1, A_ROW))


def _attn_a_kernel(q_ref, k_ref, v_ref, g_ref, tt_ref, o_ref, *, hb, nt):
    blk = pl.program_id(2)
    tasks = [(hh, t) for t in range(nt) for hh in range(hb)]

    def window(t):
        q0 = (blk * nt + t) * A_TQ
        ws = pl.multiple_of(jnp.maximum(q0 - LEFT, 0), A_TQ)
        u0 = pl.multiple_of(LEFT - jnp.minimum(q0, LEFT), A_TQ)
        return ws, u0

    def scores(i, _):
        hh, t = tasks[i]
        ws, u0 = window(t)
        s = lax.dot_general(k_ref[hh, pl.ds(ws, A_WIN), :], q_ref[hh, t * A_TQ:(t + 1) * A_TQ, :],
                            (((1,), (1,)), ((), ())), preferred_element_type=F32)
        return s + tt_ref[hh, pl.ds(u0, A_WIN), :]

    def softmax(i, s):
        p = jnp.exp2(s - jnp.max(s, axis=0, keepdims=True))
        return p.astype(BF16), jnp.sum(p, axis=0, keepdims=True)

    def output(i, p_l):
        p, l = p_l
        hh, t = tasks[i]
        ws, _ = window(t)
        ot = lax.dot_general(v_ref[hh, pl.ds(ws, A_WIN), :], p, (((0,), (0,)), ((), ())),
                             preferred_element_type=F32)
        g = g_ref[hh, t * A_TQ:(t + 1) * A_TQ, :].astype(F32)
        o_ref[t * A_TQ:(t + 1) * A_TQ, hh * LANE:(hh + 1) * LANE] = (
            (ot / l).T * g * jax.nn.sigmoid(g)).astype(o_ref.dtype)

    _emit_skewed(len(tasks), [scores, softmax, output])


def _attn_a(qkvg, tt, batch, seq, hb=4, nt=4):
    nblk = N_HEADS // hb
    tqb = nt * A_TQ
    nq = seq // tqb
    kern = functools.partial(_attn_a_kernel, hb=hb, nt=nt)

    def slab(role):
        return pl.BlockSpec((hb, seq, LANE), lambda b, h, i: (role * nblk + h, b, 0))

    def tile(role):
        return pl.BlockSpec((hb, tqb, LANE), lambda b, h, i: (role * nblk + h, b * nq + i, 0))

    return pl.pallas_call(
        kern,
        grid=(batch, nblk, nq),
        in_specs=[tile(0), slab(1), slab(2), tile(3),
                  pl.BlockSpec((hb, A_TU, A_TQ), lambda b, h, i: (h, 0, 0))],
        out_specs=pl.BlockSpec((tqb, hb * LANE), lambda b, h, i: (b * nq + i, h)),
        out_shape=jax.ShapeDtypeStruct((batch * seq, D_INNER), BF16),
        compiler_params=_cparams(3, 56),
        name="attn_a",
    )(qkvg, qkvg, qkvg, qkvg, tt)


def _outproj_kernel(a_ref, w_ref, x_ref, o_ref):
    o_ref[...] = x_ref[...] + jnp.dot(a_ref[...], w_ref[...].astype(BF16), preferred_element_type=F32)


def _outproj(a, w, layer, x, tm=1024, tn=512):
    m, k = a.shape
    n = w.shape[2]
    return pl.pallas_call(
        _outproj_kernel,
        grid=(m // tm, n // tn),
        in_specs=[pl.BlockSpec((tm, k), lambda i, j: (i, 0)),
                  pl.BlockSpec((None, k, tn), lambda i, j: (layer, 0, j)),
                  pl.BlockSpec((tm, tn), lambda i, j: (i, j))],
        out_specs=pl.BlockSpec((tm, tn), lambda i, j: (i, j)),
        out_shape=jax.ShapeDtypeStruct((m, n), F32),
        compiler_params=_cparams(2, 56),
        name="outproj",
    )(a, w, x)


def _split3(x):
    hi = x.astype(BF16)
    r1 = x - hi.astype(F32)
    mid = r1.astype(BF16)
    lo = (r1 - mid.astype(F32)).astype(BF16)
    return hi, mid, lo


def _fgate_kernel(h_ref, fw_ref, fb_ref, qx_ref, kx_ref, carry_ref, *, ts):
    @pl.when(pl.program_id(1) == 0)
    def _():
        carry_ref[...] = jnp.zeros_like(carry_ref)

    z = jnp.dot(h_ref[...], fw_ref[...].astype(BF16), preferred_element_type=F32) + fb_ref[...]
    logf = jnp.minimum(z, 0.0) - jnp.log1p(jnp.exp(-jnp.abs(z)))
    r = lax.broadcasted_iota(jnp.int32, (ts, ts), 0)
    c = lax.broadcasted_iota(jnp.int32, (ts, ts), 1)
    tri = jnp.where(r >= c, 1.0, 0.0).astype(BF16)
    cum = carry_ref[...]
    for part in _split3(logf):
        cum = cum + jnp.dot(tri, part, preferred_element_type=F32)
    carry_ref[...] = cum[ts - 1:ts, :]
    parts = jnp.concatenate(_split3(cum * LOG2E), axis=1)
    rr = lax.broadcasted_iota(jnp.int32, (3 * LANE, 2 * LANE), 0)
    cc = lax.broadcasted_iota(jnp.int32, (3 * LANE, 2 * LANE), 1)
    lane = lax.broadcasted_iota(jnp.int32, (1, 2 * LANE), 1)
    ones = jnp.where((lane >= 3) & (lane < 6) | (lane >= LANE) & (lane < LANE + 3), 1.0, 0.0)
    for hd in range(N_HEADS):
        sel = (jnp.where((cc < 3) & (rr == cc * LANE + hd), 1.0, 0.0)
               - jnp.where((cc >= LANE + 3) & (cc < LANE + 6) & (rr == (cc - LANE - 3) * LANE + hd), 1.0, 0.0))
        ex = jnp.dot(parts, sel.astype(BF16), preferred_element_type=F32) + ones
        qx_ref[hd] = ex[:, :LANE].astype(qx_ref.dtype)
        kx_ref[hd] = ex[:, LANE:].astype(kx_ref.dtype)


def _fgate(h, f_w, f_b, batch, seq, ts=512):
    m, d = h.shape
    fw = jnp.pad(f_w, ((0, 0), (0, LANE - N_HEADS)))
    fb = jnp.pad(f_b, (0, LANE - N_HEADS)).reshape(1, LANE)
    nblk = seq // ts
    ex_spec = pl.BlockSpec((N_HEADS, ts, LANE), lambda b, s: (0, b * nblk + s, 0))
    ex_shape = jax.ShapeDtypeStruct((N_HEADS, m, LANE), BF16)
    return pl.pallas_call(
        functools.partial(_fgate_kernel, ts=ts),
        grid=(batch, nblk),
        in_specs=[pl.BlockSpec((ts, d), lambda b, s: (b * nblk + s, 0)),
                  pl.BlockSpec((d, LANE), lambda b, s: (0, 0)),
                  pl.BlockSpec((1, LANE), lambda b, s: (0, 0))],
        out_specs=(ex_spec, ex_spec),
        out_shape=(ex_shape, ex_shape),
        scratch_shapes=[pltpu.VMEM((1, LANE), F32)],
        compiler_params=_cparams(2, 48),
        name="fgate",
    )(h, fw, fb)


def _attn_b_kernel(q_ref, qx_ref, k_ref, kx_ref, v_ref, g_ref, o_ref, qt_ref, vt_ref, *, hb, tq, tk):
    sub = tq // tk
    tc = tq
    ns = tq // tc
    i = pl.program_id(2)
    row = lax.broadcasted_iota(jnp.int32, (tk, tc), 0)
    col = lax.broadcasted_iota(jnp.int32, (tk, tc), 1)

    for hh in range(hb):
        qt_ref[hh] = jnp.concatenate([q_ref[hh], qx_ref[hh]], axis=1).T

    @pl.when(i == 0)
    def _():
        for hh in range(hb):
            for c in range(vt_ref.shape[1]):
                vt_ref[hh, c] = v_ref[hh, c * tk:(c + 1) * tk, :].T

    def kv_step(j, state, diag_off):
        k0 = pl.multiple_of(j * tk, tk)

        def scores(t, _):
            hh, c = divmod(t, ns)
            ka = jnp.concatenate([k_ref[hh, pl.ds(k0, tk), :], kx_ref[hh, pl.ds(k0, tk), :]], axis=1)
            s = jnp.dot(ka, qt_ref[hh, :, c * tc:(c + 1) * tc], preferred_element_type=F32)
            if diag_off is not None:
                s = jnp.where(row + (diag_off - c * tc) <= col, s, NEG_INF)
            return s

        def softmax(t, s):
            m, l, _ = state[t]
            m_new = jnp.maximum(m, jnp.max(s, axis=0, keepdims=True))
            alpha = jnp.exp2(m - m_new)
            p = jnp.exp2(s - m_new)
            return m_new, alpha * l + jnp.sum(p, axis=0, keepdims=True), alpha, p.astype(BF16)

        def accumulate(t, sm):
            m_new, l_new, alpha, p = sm
            pv = jnp.dot(vt_ref[t // ns, j], p, preferred_element_type=F32)
            return m_new, l_new, alpha * state[t][2] + pv

        return tuple(_emit_skewed(hb * ns, [scores, softmax, accumulate]))

    init = tuple((jnp.full((1, tc), NEG_INF, F32), jnp.zeros((1, tc), F32), jnp.zeros((HEAD_DIM, tc), F32))
                 for _ in range(hb * ns))
    state = lax.fori_loop(0, i * sub, lambda j, c: kv_step(j, c, None), init)
    for d in range(sub):
        state = kv_step(i * sub + d, state, d * tk)
    for t in range(hb * ns):
        hh, c = divmod(t, ns)
        _, l, acc = state[t]
        g = g_ref[hh, c * tc:(c + 1) * tc, :].astype(F32)
        o_ref[c * tc:(c + 1) * tc, hh * LANE:(hh + 1) * LANE] = (
            (acc / l).T * g * jax.nn.sigmoid(g)).astype(o_ref.dtype)


def _attn_b(qg, kv, qx, kx, batch, seq, hb=4, tq=512, tk=512):
    nblk = N_HEADS // hb
    nq = seq // tq
    kern = functools.partial(_attn_b_kernel, hb=hb, tq=tq, tk=tk)

    def slab(role):
        return pl.BlockSpec((hb, seq, LANE), lambda b, h, i: (role * nblk + h, b, 0))

    def tile(role):
        return pl.BlockSpec((hb, tq, LANE), lambda b, h, i: (role * nblk + h, b * nq + i, 0))

    return pl.pallas_call(
        kern,
        grid=(batch, nblk, nq),
        in_specs=[tile(0), tile(0), slab(0), slab(0), slab(1), tile(1)],
        out_specs=pl.BlockSpec((tq, hb * LANE), lambda b, h, i: (b * nq + i, h)),
        out_shape=jax.ShapeDtypeStruct((batch * seq, D_INNER), BF16),
        scratch_shapes=[pltpu.VMEM((hb, 2 * LANE, tq), BF16),
                        pltpu.VMEM((hb, seq // tk, LANE, tk), BF16)],
        compiler_params=_cparams(3, 56),
        name="attn_b",
    )(qg, qx, kv, kx, kv, qg)


def kernel(x, a_norm, a_w_in, a_rel_bias, a_w_out, kv_norm, kv_w, f_w, f_b, b_norm, b_w_in, b_w_out, final_norm):
    batch, seq, d = x.shape
    xr = x.reshape(batch * seq, d)
    for layer in range(a_w_in.shape[0]):
        h = _rmsnorm(xr, a_norm[layer], BF16)
        qkvg = _proj(h, a_w_in, layer, D_INNER)
        tt = _bias_table(a_rel_bias[layer])
        a = _attn_a(qkvg, tt, batch, seq)
        xr = _outproj(a, a_w_out, layer, xr)
    h_kv = _rmsnorm(xr, kv_norm, BF16)
    kv = _proj(h_kv, kv_w[None], 0, 0)
    qx, kx = _fgate(h_kv, f_w, f_b, batch, seq)
    for layer in range(b_w_in.shape[0]):
        h = _rmsnorm(xr, b_norm[layer], BF16)
        qg = _proj(h, b_w_in, layer, D_INNER)
        a = _attn_b(qg, kv, qx, kx, batch, seq)
        xr = _outproj(a, b_w_out, layer, xr)
    return _rmsnorm(xr, final_norm, x.dtype).reshape(batch, seq, d)
```

```python
import functools

import jax
import jax.numpy as jnp
from jax import lax
from jax.experimental import pallas as pl
from jax.experimental.pallas import tpu as pltpu

D_MODEL = 2048
D_INNER = 4096
HEAD_DIM = 128
N_HEADS = D_INNER // HEAD_DIM
CHUNK = 64
N_LEFT_CHUNKS = 8
LEFT = N_LEFT_CHUNKS * CHUNK
BAND = LEFT + CHUNK
REL_CLIP = 256
EPS = 1e-6
NEG_INF = -1e30
LOG2E = 1.4426950408889634
QK_SCALE = HEAD_DIM ** -0.5 * LOG2E

LANE = 128
MIB = 1024 * 1024

F32 = jnp.float32
BF16 = jnp.bfloat16

A_TQ = 4 * CHUNK
A_WIN = LEFT + A_TQ
A_TU = LEFT + A_WIN
A_ROW = A_TU + A_TQ


def _cparams(n_axes, vmem_mib):
    return pltpu.CompilerParams(dimension_semantics=("arbitrary",) * n_axes,
                                vmem_limit_bytes=vmem_mib * MIB)


def _rmsnorm_kernel(x_ref, g_ref, o_ref):
    x = x_ref[...]
    ms = jnp.mean(x * x, axis=-1, keepdims=True)
    o_ref[...] = (x * lax.rsqrt(ms + EPS) * g_ref[...]).astype(o_ref.dtype)


def _rmsnorm(x, gain, out_dtype, tm=512):
    m, d = x.shape
    return pl.pallas_call(
        _rmsnorm_kernel,
        grid=(m // tm,),
        in_specs=[pl.BlockSpec((tm, d), lambda i: (i, 0)),
                  pl.BlockSpec((1, d), lambda i: (0, 0))],
        out_specs=pl.BlockSpec((tm, d), lambda i: (i, 0)),
        out_shape=jax.ShapeDtypeStruct((m, d), out_dtype),
        compiler_params=_cparams(1, 32),
        name="rmsnorm",
    )(x, gain.reshape(1, d))


def _proj_kernel(h_ref, w_ref, o_ref, *, n_scaled_tiles, scale):
    j = pl.program_id(1)
    acc = jnp.dot(h_ref[...], w_ref[...].astype(BF16), preferred_element_type=F32)
    if n_scaled_tiles:
        acc = acc * jnp.where(j < n_scaled_tiles, scale, 1.0).astype(F32)
    for s in range(o_ref.shape[0]):
        o_ref[s] = acc[:, s * LANE:(s + 1) * LANE].astype(o_ref.dtype)


def _proj(h, w, layer, n_scaled_cols, tm=2048, tn=512):
    m, k = h.shape
    n = w.shape[2]
    kern = functools.partial(_proj_kernel, n_scaled_tiles=n_scaled_cols // tn, scale=QK_SCALE)
    return pl.pallas_call(
        kern,
        grid=(m // tm, n // tn),
        in_specs=[pl.BlockSpec((tm, k), lambda i, j: (i, 0)),
                  pl.BlockSpec((None, k, tn), lambda i, j: (layer, 0, j))],
        out_specs=pl.BlockSpec((tn // LANE, tm, LANE), lambda i, j: (j, i, 0)),
        out_shape=jax.ShapeDtypeStruct((n // LANE, m, LANE), BF16),
        compiler_params=_cparams(2, 48),
        name="proj",
    )(h, w)


def _emit_skewed(n_tasks, stages):
    k = len(stages)
    order = [0] + list(range(k - 1, 0, -1))
    vals, out = {}, [None] * n_tasks
    for step in range(n_tasks + k - 1):
        for si in order:
            t = step - si
            if 0 <= t < n_tasks:
                res = stages[si](t, vals.pop((t, si - 1)) if si else None)
                if si == k - 1:
                    out[t] = res
                else:
                    vals[(t, si)] = res
    return out


def _bias_kernel(w_ref, o_ref):
    x = jnp.broadcast_to(w_ref[0], (A_TQ, A_ROW))
    r = pltpu.roll(x, A_ROW - A_TQ, 1, stride=1, stride_axis=0)
    i = lax.broadcasted_iota(jnp.int32, (A_TQ, A_TU), 0)
    u = lax.broadcasted_iota(jnp.int32, (A_TQ, A_TU), 1)
    lo = (i // CHUNK) * CHUNK
    valid = (u >= lo) & (u < lo + BAND)
    o_ref[0] = jnp.where(valid, r[:, :A_TU] * LOG2E, NEG_INF).T


def _bias_table(rel_bias):
    h = rel_bias.shape[0]
    n_hi = LEFT + A_TQ - REL_CLIP
    n_lo = A_ROW - n_hi - (2 * REL_CLIP + 1)
    row = jnp.concatenate([jnp.broadcast_to(rel_bias[:, -1:], (h, n_hi)),
                           rel_bias[:, ::-1],
                           jnp.broadcast_to(rel_bias[:, :1], (h, n_lo))], axis=1)
    return pl.pallas_call(
        _bias_kernel,
        grid=(h,),
        in_specs=[pl.BlockSpec((1, 1, A_ROW), lambda i: (i, 0, 0))],
        out_specs=pl.BlockSpec((1, A_TU, A_TQ), lambda i: (i, 0, 0)),
        out_shape=jax.ShapeDtypeStruct((h, A_TU, A_TQ), F32),
        compiler_params=_cparams(1, 32),
        name="bias_table",
    )(row.reshape(h, 1, A_ROW))


def _attn_a_kernel(q_ref, k_ref, v_ref, g_ref, tt_ref, o_ref, *, hb, nt):
    blk = pl.program_id(2)
    tasks = [(hh, t) for t in range(nt) for hh in range(hb)]

    def window(t):
        q0 = (blk * nt + t) * A_TQ
        ws = pl.multiple_of(jnp.maximum(q0 - LEFT, 0), A_TQ)
        u0 = pl.multiple_of(LEFT - jnp.minimum(q0, LEFT), A_TQ)
        return ws, u0

    def scores(i, _):
        hh, t = tasks[i]
        ws, u0 = window(t)
        s = lax.dot_general(k_ref[hh, pl.ds(ws, A_WIN), :], q_ref[hh, t * A_TQ:(t + 1) * A_TQ, :],
                            (((1,), (1,)), ((), ())), preferred_element_type=F32)
        return s + tt_ref[hh, pl.ds(u0, A_WIN), :]

    def softmax(i, s):
        p = jnp.exp2(s - jnp.max(s, axis=0, keepdims=True))
        return p.astype(BF16), jnp.sum(p, axis=0, keepdims=True)

    def output(i, p_l):
        p, l = p_l
        hh, t = tasks[i]
        ws, _ = window(t)
        ot = lax.dot_general(v_ref[hh, pl.ds(ws, A_WIN), :], p, (((0,), (0,)), ((), ())),
                             preferred_element_type=F32)
        g = g_ref[hh, t * A_TQ:(t + 1) * A_TQ, :].astype(F32)
        o_ref[t * A_TQ:(t + 1) * A_TQ, hh * LANE:(hh + 1) * LANE] = (
            (ot / l).T * g * jax.nn.sigmoid(g)).astype(o_ref.dtype)

    _emit_skewed(len(tasks), [scores, softmax, output])


def _attn_a(qkvg, tt, batch, seq, hb=4, nt=4):
    nblk = N_HEADS // hb
    tqb = nt * A_TQ
    nq = seq // tqb
    kern = functools.partial(_attn_a_kernel, hb=hb, nt=nt)

    def slab(role):
        return pl.BlockSpec((hb, seq, LANE), lambda b, h, i: (role * nblk + h, b, 0))

    def tile(role):
        return pl.BlockSpec((hb, tqb, LANE), lambda b, h, i: (role * nblk + h, b * nq + i, 0))

    return pl.pallas_call(
        kern,
        grid=(batch, nblk, nq),
        in_specs=[tile(0), slab(1), slab(2), tile(3),
                  pl.BlockSpec((hb, A_TU, A_TQ), lambda b, h, i: (h, 0, 0))],
        out_specs=pl.BlockSpec((tqb, hb * LANE), lambda b, h, i: (b * nq + i, h)),
        out_shape=jax.ShapeDtypeStruct((batch * seq, D_INNER), BF16),
        compiler_params=_cparams(3, 56),
        name="attn_a",
    )(qkvg, qkvg, qkvg, qkvg, tt)


def _outproj_kernel(a_ref, w_ref, x_ref, o_ref):
    o_ref[...] = x_ref[...] + jnp.dot(a_ref[...], w_ref[...].astype(BF16), preferred_element_type=F32)


def _outproj(a, w, layer, x, tm=1024, tn=512):
    m, k = a.shape
    n = w.shape[2]
    return pl.pallas_call(
        _outproj_kernel,
        grid=(m // tm, n // tn),
        in_specs=[pl.BlockSpec((tm, k), lambda i, j: (i, 0)),
                  pl.BlockSpec((None, k, tn), lambda i, j: (layer, 0, j)),
                  pl.BlockSpec((tm, tn), lambda i, j: (i, j))],
        out_specs=pl.BlockSpec((tm, tn), lambda i, j: (i, j)),
        out_shape=jax.ShapeDtypeStruct((m, n), F32),
        compiler_params=_cparams(2, 56),
        name="outproj",
    )(a, w, x)


def _split3(x):
    hi = x.astype(BF16)
    r1 = x - hi.astype(F32)
    mid = r1.astype(BF16)
    lo = (r1 - mid.astype(F32)).astype(BF16)
    return hi, mid, lo


def _fgate_kernel(h_ref, fw_ref, fb_ref, qx_ref, kx_ref, carry_ref, *, ts):
    @pl.when(pl.program_id(1) == 0)
    def _():
        carry_ref[...] = jnp.zeros_like(carry_ref)

    z = jnp.dot(h_ref[...], fw_ref[...].astype(BF16), preferred_element_type=F32) + fb_ref[...]
    logf = jnp.minimum(z, 0.0) - jnp.log1p(jnp.exp(-jnp.abs(z)))
    r = lax.broadcasted_iota(jnp.int32, (ts, ts), 0)
    c = lax.broadcasted_iota(jnp.int32, (ts, ts), 1)
    tri = jnp.where(r >= c, 1.0, 0.0).astype(BF16)
    cum = carry_ref[...]
    for part in _split3(logf):
        cum = cum + jnp.dot(tri, part, preferred_element_type=F32)
    carry_ref[...] = cum[ts - 1:ts, :]
    parts = jnp.concatenate(_split3(cum * LOG2E), axis=1)
    rr = lax.broadcasted_iota(jnp.int32, (3 * LANE, 2 * LANE), 0)
    cc = lax.broadcasted_iota(jnp.int32, (3 * LANE, 2 * LANE), 1)
    lane = lax.broadcasted_iota(jnp.int32, (1, 2 * LANE), 1)
    ones = jnp.where((lane >= 3) & (lane < 6) | (lane >= LANE) & (lane < LANE + 3), 1.0, 0.0)
    for hd in range(N_HEADS):
        sel = (jnp.where((cc < 3) & (rr == cc * LANE + hd), 1.0, 0.0)
               - jnp.where((cc >= LANE + 3) & (cc < LANE + 6) & (rr == (cc - LANE - 3) * LANE + hd), 1.0, 0.0))
        ex = jnp.dot(parts, sel.astype(BF16), preferred_element_type=F32) + ones
        qx_ref[hd] = ex[:, :LANE].astype(qx_ref.dtype)
        kx_ref[hd] = ex[:, LANE:].astype(kx_ref.dtype)


def _fgate(h, f_w, f_b, batch, seq, ts=512):
    m, d = h.shape
    fw = jnp.pad(f_w, ((0, 0), (0, LANE - N_HEADS)))
    fb = jnp.pad(f_b, (0, LANE - N_HEADS)).reshape(1, LANE)
    nblk = seq // ts
    ex_spec = pl.BlockSpec((N_HEADS, ts, LANE), lambda b, s: (0, b * nblk + s, 0))
    ex_shape = jax.ShapeDtypeStruct((N_HEADS, m, LANE), BF16)
    return pl.pallas_call(
        functools.partial(_fgate_kernel, ts=ts),
        grid=(batch, nblk),
        in_specs=[pl.BlockSpec((ts, d), lambda b, s: (b * nblk + s, 0)),
                  pl.BlockSpec((d, LANE), lambda b, s: (0, 0)),
                  pl.BlockSpec((1, LANE), lambda b, s: (0, 0))],
        out_specs=(ex_spec, ex_spec),
        out_shape=(ex_shape, ex_shape),
        scratch_shapes=[pltpu.VMEM((1, LANE), F32)],
        compiler_params=_cparams(2, 48),
        name="fgate",
    )(h, fw, fb)


def _attn_b_kernel(q_ref, qx_ref, k_ref, kx_ref, v_ref, g_ref, o_ref, qt_ref, vt_ref, *, hb, tq, tk):
    assert tq == tk
    i = pl.program_id(2)
    row = lax.broadcasted_iota(jnp.int32, (tk, tq), 0)
    col = lax.broadcasted_iota(jnp.int32, (tk, tq), 1)

    for hh in range(hb):
        qt_ref[hh] = jnp.concatenate([q_ref[hh], qx_ref[hh]], axis=1).T

    @pl.when(i == 0)
    def _():
        for hh in range(hb):
            for c in range(vt_ref.shape[1]):
                vt_ref[hh, c] = v_ref[hh, c * tk:(c + 1) * tk, :].T

    def kv_steps(js, state, diag_off):
        state = list(state)
        tasks = [(j, hh) for j in js for hh in range(hb)]

        def scores(t, _):
            j, hh = tasks[t]
            k0 = pl.multiple_of(j * tk, tk)
            ka = jnp.concatenate([k_ref[hh, pl.ds(k0, tk), :], kx_ref[hh, pl.ds(k0, tk), :]], axis=1)
            s = jnp.dot(ka, qt_ref[hh], preferred_element_type=F32)
            if diag_off is not None:
                s = jnp.where(row + diag_off <= col, s, NEG_INF)
            return s

        def softmax(t, s):
            m, l, _ = state[tasks[t][1]]
            m_new = jnp.maximum(m, jnp.max(s, axis=0, keepdims=True))
            alpha = jnp.exp2(m - m_new)
            p = jnp.exp2(s - m_new)
            return m_new, alpha * l + jnp.sum(p, axis=0, keepdims=True), alpha, p.astype(BF16)

        def accumulate(t, sm):
            j, hh = tasks[t]
            m_new, l_new, alpha, p = sm
            pv = jnp.dot(vt_ref[hh, j], p, preferred_element_type=F32)
            state[hh] = (m_new, l_new, alpha * state[hh][2] + pv)

        _emit_skewed(len(tasks), [scores, lambda t, s: s, softmax, accumulate])
        return tuple(state)

    init = tuple((jnp.full((1, tq), NEG_INF, F32), jnp.zeros((1, tq), F32), jnp.zeros((HEAD_DIM, tq), F32))
                 for _ in range(hb))
    state = lax.fori_loop(0, i // 2, lambda jp, c: kv_steps([2 * jp, 2 * jp + 1], c, None), init)
    state = lax.fori_loop(0, i % 2, lambda _, c: kv_steps([i - 1], c, None), state)
    state = kv_steps([i], state, 0)
    for hh in range(hb):
        _, l, acc = state[hh]
        g = g_ref[hh].astype(F32)
        o_ref[:, hh * LANE:(hh + 1) * LANE] = ((acc / l).T * g * jax.nn.sigmoid(g)).astype(o_ref.dtype)


def _attn_b(qg, kv, qx, kx, batch, seq, hb=4, tq=512, tk=512):
    nblk = N_HEADS // hb
    nq = seq // tq
    kern = functools.partial(_attn_b_kernel, hb=hb, tq=tq, tk=tk)

    def slab(role):
        return pl.BlockSpec((hb, seq, LANE), lambda b, h, i: (role * nblk + h, b, 0))

    def tile(role):
        return pl.BlockSpec((hb, tq, LANE), lambda b, h, i: (role * nblk + h, b * nq + i, 0))

    return pl.pallas_call(
        kern,
        grid=(batch, nblk, nq),
        in_specs=[tile(0), tile(0), slab(0), slab(0), slab(1), tile(1)],
        out_specs=pl.BlockSpec((tq, hb * LANE), lambda b, h, i: (b * nq + i, h)),
        out_shape=jax.ShapeDtypeStruct((batch * seq, D_INNER), BF16),
        scratch_shapes=[pltpu.VMEM((hb, 2 * LANE, tq), BF16),
                        pltpu.VMEM((hb, seq // tk, LANE, tk), BF16)],
        compiler_params=_cparams(3, 56),
        name="attn_b",
    )(qg, qx, kv, kx, kv, qg)


def kernel(x, a_norm, a_w_in, a_rel_bias, a_w_out, kv_norm, kv_w, f_w, f_b, b_norm, b_w_in, b_w_out, final_norm):
    batch, seq, d = x.shape
    xr = x.reshape(batch * seq, d)
    for layer in range(a_w_in.shape[0]):
        h = _rmsnorm(xr, a_norm[layer], BF16)
        qkvg = _proj(h, a_w_in, layer, D_INNER)
        tt = _bias_table(a_rel_bias[layer])
        a = _attn_a(qkvg, tt, batch, seq)
        xr = _outproj(a, a_w_out, layer, xr)
    h_kv = _rmsnorm(xr, kv_norm, BF16)
    kv = _proj(h_kv, kv_w[None], 0, 0)
    qx, kx = _fgate(h_kv, f_w, f_b, batch, seq)
    for layer in range(b_w_in.shape[0]):
        h = _rmsnorm(xr, b_norm[layer], BF16)
        qg = _proj(h, b_w_in, layer, D_INNER)
        a = _attn_b(qg, kv, qx, kx, batch, seq)
        xr = _outproj(a, b_w_out, layer, xr)
    return _rmsnorm(xr, final_norm, x.dtype).reshape(batch, seq, d)
```

```python
import functools

import jax
import jax.numpy as jnp
from jax import lax
from jax.experimental import pallas as pl
from jax.experimental.pallas import tpu as pltpu

D_MODEL = 2048
D_INNER = 4096
HEAD_DIM = 128
N_HEADS = D_INNER // HEAD_DIM
CHUNK = 64
N_LEFT_CHUNKS = 8
LEFT = N_LEFT_CHUNKS * CHUNK
BAND = LEFT + CHUNK
REL_CLIP = 256
EPS = 1e-6
NEG_INF = -1e30
LOG2E = 1.4426950408889634
QK_SCALE = HEAD_DIM ** -0.5 * LOG2E

LANE = 128
MIB = 1024 * 1024

F32 = jnp.float32
BF16 = jnp.bfloat16

A_TQ = 4 * CHUNK
A_WIN = LEFT + A_TQ
A_TU = LEFT + A_WIN
A_ROW = A_TU + A_TQ

X_LANES = 8
X_HEADS = LANE // X_LANES
X_GROUPS = N_HEADS // X_HEADS


def _cparams(n_axes, vmem_mib):
    return pltpu.CompilerParams(dimension_semantics=("arbitrary",) * n_axes,
                                vmem_limit_bytes=vmem_mib * MIB)


def _rmsnorm_kernel(x_ref, g_ref, o_ref):
    x = x_ref[...]
    ms = jnp.mean(x * x, axis=-1, keepdims=True)
    o_ref[...] = (x * lax.rsqrt(ms + EPS) * g_ref[...]).astype(o_ref.dtype)


def _rmsnorm(x, gain, out_dtype, tm=512):
    m, d = x.shape
    return pl.pallas_call(
        _rmsnorm_kernel,
        grid=(m // tm,),
        in_specs=[pl.BlockSpec((tm, d), lambda i: (i, 0)),
                  pl.BlockSpec((1, d), lambda i: (0, 0))],
        out_specs=pl.BlockSpec((tm, d), lambda i: (i, 0)),
        out_shape=jax.ShapeDtypeStruct((m, d), out_dtype),
        compiler_params=_cparams(1, 32),
        name="rmsnorm",
    )(x, gain.reshape(1, d))


def _proj_kernel(h_ref, w_ref, o_ref, *, n_scaled_tiles, scale):
    j = pl.program_id(1)
    acc = jnp.dot(h_ref[...], w_ref[...].astype(BF16), preferred_element_type=F32)
    if n_scaled_tiles:
        acc = acc * jnp.where(j < n_scaled_tiles, scale, 1.0).astype(F32)
    for s in range(o_ref.shape[0]):
        o_ref[s] = acc[:, s * LANE:(s + 1) * LANE].astype(o_ref.dtype)


def _proj(h, w, layer, n_scaled_cols, tm=2048, tn=512):
    m, k = h.shape
    n = w.shape[2]
    kern = functools.partial(_proj_kernel, n_scaled_tiles=n_scaled_cols // tn, scale=QK_SCALE)
    return pl.pallas_call(
        kern,
        grid=(m // tm, n // tn),
        in_specs=[pl.BlockSpec((tm, k), lambda i, j: (i, 0)),
                  pl.BlockSpec((None, k, tn), lambda i, j: (layer, 0, j))],
        out_specs=pl.BlockSpec((tn // LANE, tm, LANE), lambda i, j: (j, i, 0)),
        out_shape=jax.ShapeDtypeStruct((n // LANE, m, LANE), BF16),
        compiler_params=_cparams(2, 48),
        name="proj",
    )(h, w)


def _emit_skewed(n_tasks, stages):
    k = len(stages)
    order = [0] + list(range(k - 1, 0, -1))
    vals, out = {}, [None] * n_tasks
    for step in range(n_tasks + k - 1):
        for si in order:
            t = step - si
            if 0 <= t < n_tasks:
                res = stages[si](t, vals.pop((t, si - 1)) if si else None)
                if si == k - 1:
                    out[t] = res
                else:
                    vals[(t, si)] = res
    return out


def _bias_kernel(w_ref, o_ref):
    x = jnp.broadcast_to(w_ref[0], (A_TQ, A_ROW))
    r = pltpu.roll(x, A_ROW - A_TQ, 1, stride=1, stride_axis=0)
    i = lax.broadcasted_iota(jnp.int32, (A_TQ, A_TU), 0)
    u = lax.broadcasted_iota(jnp.int32, (A_TQ, A_TU), 1)
    lo = (i // CHUNK) * CHUNK
    valid = (u >= lo) & (u < lo + BAND)
    o_ref[0] = jnp.where(valid, r[:, :A_TU] * LOG2E, NEG_INF).T


def _bias_table(rel_bias):
    h = rel_bias.shape[0]
    n_hi = LEFT + A_TQ - REL_CLIP
    n_lo = A_ROW - n_hi - (2 * REL_CLIP + 1)
    row = jnp.concatenate([jnp.broadcast_to(rel_bias[:, -1:], (h, n_hi)),
                           rel_bias[:, ::-1],
                           jnp.broadcast_to(rel_bias[:, :1], (h, n_lo))], axis=1)
    return pl.pallas_call(
        _bias_kernel,
        grid=(h,),
        in_specs=[pl.BlockSpec((1, 1, A_ROW), lambda i: (i, 0, 0))],
        out_specs=pl.BlockSpec((1, A_TU, A_TQ), lambda i: (i, 0, 0)),
        out_shape=jax.ShapeDtypeStruct((h, A_TU, A_TQ), F32),
        compiler_params=_cparams(1, 32),
        name="bias_table",
    )(row.reshape(h, 1, A_ROW))


def _attn_a_kernel(q_ref, k_ref, v_ref, g_ref, tt_ref, o_ref, *, hb, nt):
    blk = pl.program_id(2)
    tasks = [(hh, t) for t in range(nt) for hh in range(hb)]

    def window(t):
        q0 = (blk * nt + t) * A_TQ
        ws = pl.multiple_of(jnp.maximum(q0 - LEFT, 0), A_TQ)
        u0 = pl.multiple_of(LEFT - jnp.minimum(q0, LEFT), A_TQ)
        return ws, u0

    def scores(i, _):
        hh, t = tasks[i]
        ws, u0 = window(t)
        s = lax.dot_general(k_ref[hh, pl.ds(ws, A_WIN), :], q_ref[hh, t * A_TQ:(t + 1) * A_TQ, :],
                            (((1,), (1,)), ((), ())), preferred_element_type=F32)
        return s + tt_ref[hh, pl.ds(u0, A_WIN), :]

    def softmax(i, s):
        p = jnp.exp2(s - jnp.max(s, axis=0, keepdims=True))
        return p.astype(BF16), jnp.sum(p, axis=0, keepdims=True)

    def output(i, p_l):
        p, l = p_l
        hh, t = tasks[i]
        ws, _ = window(t)
        ot = lax.dot_general(v_ref[hh, pl.ds(ws, A_WIN), :], p, (((0,), (0,)), ((), ())),
                             preferred_element_type=F32)
        g = g_ref[hh, t * A_TQ:(t + 1) * A_TQ, :].astype(F32)
        o_ref[t * A_TQ:(t + 1) * A_TQ, hh * LANE:(hh + 1) * LANE] = (
            (ot / l).T * g * jax.nn.sigmoid(g)).astype(o_ref.dtype)

    _emit_skewed(len(tasks), [scores, softmax, output])


def _attn_a(qkvg, tt, batch, seq, hb=4, nt=8):
    nblk = N_HEADS // hb
    tqb = nt * A_TQ
    nq = seq // tqb
    kern = functools.partial(_attn_a_kernel, hb=hb, nt=nt)

    def slab(role):
        return pl.BlockSpec((hb, seq, LANE), lambda b, h, i: (role * nblk + h, b, 0))

    def tile(role):
        return pl.BlockSpec((hb, tqb, LANE), lambda b, h, i: (role * nblk + h, b * nq + i, 0))

    return pl.pallas_call(
        kern,
        grid=(batch, nblk, nq),
        in_specs=[tile(0), slab(1), slab(2), tile(3),
                  pl.BlockSpec((hb, A_TU, A_TQ), lambda b, h, i: (h, 0, 0))],
        out_specs=pl.BlockSpec((tqb, hb * LANE), lambda b, h, i: (b * nq + i, h)),
        out_shape=jax.ShapeDtypeStruct((batch * seq, D_INNER), BF16),
        compiler_params=_cparams(3, 56),
        name="attn_a",
    )(qkvg, qkvg, qkvg, qkvg, tt)


def _outproj_kernel(a_ref, w_ref, x_ref, o_ref):
    o_ref[...] = x_ref[...] + jnp.dot(a_ref[...], w_ref[...].astype(BF16), preferred_element_type=F32)


def _outproj(a, w, layer, x, tm=1024, tn=512):
    m, k = a.shape
    n = w.shape[2]
    return pl.pallas_call(
        _outproj_kernel,
        grid=(m // tm, n // tn),
        in_specs=[pl.BlockSpec((tm, k), lambda i, j: (i, 0)),
                  pl.BlockSpec((None, k, tn), lambda i, j: (layer, 0, j)),
                  pl.BlockSpec((tm, tn), lambda i, j: (i, j))],
        out_specs=pl.BlockSpec((tm, tn), lambda i, j: (i, j)),
        out_shape=jax.ShapeDtypeStruct((m, n), F32),
        compiler_params=_cparams(2, 56),
        name="outproj",
    )(a, w, x)


def _split3(x):
    hi = x.astype(BF16)
    r1 = x - hi.astype(F32)
    mid = r1.astype(BF16)
    lo = (r1 - mid.astype(F32)).astype(BF16)
    return hi, mid, lo


def _fgate_kernel(h_ref, fw_ref, fb_ref, qx_ref, kx_ref, carry_ref, *, ts):
    @pl.when(pl.program_id(1) == 0)
    def _():
        carry_ref[...] = jnp.zeros_like(carry_ref)

    z = jnp.dot(h_ref[...], fw_ref[...].astype(BF16), preferred_element_type=F32) + fb_ref[...]
    logf = jnp.minimum(z, 0.0) - jnp.log1p(jnp.exp(-jnp.abs(z)))
    r = lax.broadcasted_iota(jnp.int32, (ts, ts), 0)
    c = lax.broadcasted_iota(jnp.int32, (ts, ts), 1)
    tri = jnp.where(r >= c, 1.0, 0.0).astype(BF16)
    cum = carry_ref[...]
    for part in _split3(logf):
        cum = cum + jnp.dot(tri, part, preferred_element_type=F32)
    carry_ref[...] = cum[ts - 1:ts, :]
    parts = jnp.concatenate(_split3(cum * LOG2E), axis=1)
    rr = lax.broadcasted_iota(jnp.int32, (3 * LANE, 2 * LANE), 0)
    cc = lax.broadcasted_iota(jnp.int32, (3 * LANE, 2 * LANE), 1)
    e = cc % X_LANES
    is_q = cc < LANE
    lane = lax.broadcasted_iota(jnp.int32, (1, 2 * LANE), 1)
    le = lane % X_LANES
    ones = jnp.where((lane < LANE) & (le >= 3) & (le < 6) | (lane >= LANE) & (le < 3), 1.0, 0.0)
    for grp in range(X_GROUPS):
        head = grp * X_HEADS + (cc % LANE) // X_LANES
        sel = (jnp.where(is_q & (e < 3) & (rr == e * LANE + head), 1.0, 0.0)
               - jnp.where(~is_q & (e >= 3) & (e < 6) & (rr == (e - 3) * LANE + head), 1.0, 0.0))
        ex = jnp.dot(parts, sel.astype(BF16), preferred_element_type=F32) + ones
        qx_ref[grp] = ex[:, :LANE].astype(qx_ref.dtype)
        kx_ref[grp] = ex[:, LANE:].astype(kx_ref.dtype)


def _fgate(h, f_w, f_b, batch, seq, ts=512):
    m, d = h.shape
    fw = jnp.pad(f_w, ((0, 0), (0, LANE - N_HEADS)))
    fb = jnp.pad(f_b, (0, LANE - N_HEADS)).reshape(1, LANE)
    nblk = seq // ts
    ex_spec = pl.BlockSpec((X_GROUPS, ts, LANE), lambda b, s: (0, b * nblk + s, 0))
    ex_shape = jax.ShapeDtypeStruct((X_GROUPS, m, LANE), BF16)
    return pl.pallas_call(
        functools.partial(_fgate_kernel, ts=ts),
        grid=(batch, nblk),
        in_specs=[pl.BlockSpec((ts, d), lambda b, s: (b * nblk + s, 0)),
                  pl.BlockSpec((d, LANE), lambda b, s: (0, 0)),
                  pl.BlockSpec((1, LANE), lambda b, s: (0, 0))],
        out_specs=(ex_spec, ex_spec),
        out_shape=(ex_shape, ex_shape),
        scratch_shapes=[pltpu.VMEM((1, LANE), F32)],
        compiler_params=_cparams(2, 48),
        name="fgate",
    )(h, fw, fb)


def _attn_b_kernel(q_ref, qx_ref, k_ref, kx_ref, v_ref, g_ref, o_ref, qt_ref, ka_ref, vt_ref, *, hb, tq, tk):
    assert tq == tk
    i = pl.program_id(2)
    row = lax.broadcasted_iota(jnp.int32, (tk, tq), 0)
    col = lax.broadcasted_iota(jnp.int32, (tk, tq), 1)

    for hh in range(hb):
        qt_ref[hh] = jnp.concatenate([q_ref[hh], qx_ref[...]], axis=1).T

    @pl.when(i == 0)
    def _():
        lane_head = lax.broadcasted_iota(jnp.int32, (tk, LANE), 1) // X_LANES
        for hh in range(hb):
            own = lane_head == (pl.program_id(1) * hb + hh) % X_HEADS
            for c in range(vt_ref.shape[1]):
                rows = slice(c * tk, (c + 1) * tk)
                ka_ref[hh, rows, :LANE] = k_ref[hh, rows, :]
                ka_ref[hh, rows, LANE:] = jnp.where(own, kx_ref[rows, :], jnp.zeros((), BF16))
                vt_ref[hh, c] = v_ref[hh, rows, :].T

    def kv_steps(js, state, diag_off):
        state = list(state)
        tasks = [(j, hh) for j in js for hh in range(hb)]

        def scores(t, _):
            j, hh = tasks[t]
            k0 = pl.multiple_of(j * tk, tk)
            s = jnp.dot(ka_ref[hh, pl.ds(k0, tk), :], qt_ref[hh], preferred_element_type=F32)
            if diag_off is not None:
                s = jnp.where(row + diag_off <= col, s, NEG_INF)
            return s

        def softmax(t, s):
            m, l, _ = state[tasks[t][1]]
            m_new = jnp.maximum(m, jnp.max(s, axis=0, keepdims=True))
            alpha = jnp.exp2(m - m_new)
            p = jnp.exp2(s - m_new)
            return m_new, alpha * l + jnp.sum(p, axis=0, keepdims=True), alpha, p.astype(BF16)

        def accumulate(t, sm):
            j, hh = tasks[t]
            m_new, l_new, alpha, p = sm
            pv = jnp.dot(vt_ref[hh, j], p, preferred_element_type=F32)
            state[hh] = (m_new, l_new, alpha * state[hh][2] + pv)

        _emit_skewed(len(tasks), [scores, lambda t, s: s, softmax, accumulate])
        return tuple(state)

    init = tuple((jnp.full((1, tq), NEG_INF, F32), jnp.zeros((1, tq), F32), jnp.zeros((HEAD_DIM, tq), F32))
                 for _ in range(hb))
    state = lax.fori_loop(0, i // 2, lambda jp, c: kv_steps([2 * jp, 2 * jp + 1], c, None), init)
    state = lax.fori_loop(0, i % 2, lambda _, c: kv_steps([i - 1], c, None), state)
    state = kv_steps([i], state, 0)
    for hh in range(hb):
        _, l, acc = state[hh]
        g = g_ref[hh].astype(F32)
        o_ref[:, hh * LANE:(hh + 1) * LANE] = ((acc / l).T * g * jax.nn.sigmoid(g)).astype(o_ref.dtype)


def _attn_b(qg, kv, qx, kx, batch, seq, hb=4, tq=512, tk=512):
    nblk = N_HEADS // hb
    nq = seq // tq
    kern = functools.partial(_attn_b_kernel, hb=hb, tq=tq, tk=tk)

    def slab(role):
        return pl.BlockSpec((hb, seq, LANE), lambda b, h, i: (role * nblk + h, b, 0))

    def tile(role):
        return pl.BlockSpec((hb, tq, LANE), lambda b, h, i: (role * nblk + h, b * nq + i, 0))

    return pl.pallas_call(
        kern,
        grid=(batch, nblk, nq),
        in_specs=[tile(0), pl.BlockSpec((None, tq, LANE), lambda b, h, i: (h * hb // X_HEADS, b * nq + i, 0)),
                  slab(0), pl.BlockSpec((None, seq, LANE), lambda b, h, i: (h * hb // X_HEADS, b, 0)),
                  slab(1), tile(1)],
        out_specs=pl.BlockSpec((tq, hb * LANE), lambda b, h, i: (b * nq + i, h)),
        out_shape=jax.ShapeDtypeStruct((batch * seq, D_INNER), BF16),
        scratch_shapes=[pltpu.VMEM((hb, 2 * LANE, tq), BF16),
                        pltpu.VMEM((hb, seq, 2 * LANE), BF16),
                        pltpu.VMEM((hb, seq // tk, LANE, tk), BF16)],
        compiler_params=_cparams(3, 56),
        name="attn_b",
    )(qg, qx, kv, kx, kv, qg)


def kernel(x, a_norm, a_w_in, a_rel_bias, a_w_out, kv_norm, kv_w, f_w, f_b, b_norm, b_w_in, b_w_out, final_norm):
    batch, seq, d = x.shape
    xr = x.reshape(batch * seq, d)
    for layer in range(a_w_in.shape[0]):
        h = _rmsnorm(xr, a_norm[layer], BF16)
        qkvg = _proj(h, a_w_in, layer, D_INNER)
        tt = _bias_table(a_rel_bias[layer])
        a = _attn_a(qkvg, tt, batch, seq)
        xr = _outproj(a, a_w_out, layer, xr)
    h_kv = _rmsnorm(xr, kv_norm, BF16)
    kv = _proj(h_kv, kv_w[None], 0, 0)
    qx, kx = _fgate(h_kv, f_w, f_b, batch, seq)
    for layer in range(b_w_in.shape[0]):
        h = _rmsnorm(xr, b_norm[layer], BF16)
        qg = _proj(h, b_w_in, layer, D_INNER)
        a = _attn_b(qg, kv, qx, kx, batch, seq)
        xr = _outproj(a, b_w_out, layer, xr)
    return _rmsnorm(xr, final_norm, x.dtype).reshape(batch, seq, d)
```

```python
import functools

import jax
import jax.numpy as jnp
from jax import lax
from jax.experimental import pallas as pl
from jax.experimental.pallas import tpu as pltpu

D_MODEL = 2048
D_INNER = 4096
HEAD_DIM = 128
N_HEADS = D_INNER // HEAD_DIM
CHUNK = 64
N_LEFT_CHUNKS = 8
LEFT = N_LEFT_CHUNKS * CHUNK
BAND = LEFT + CHUNK
REL_CLIP = 256
EPS = 1e-6
NEG_INF = -1e30
LOG2E = 1.4426950408889634
QK_SCALE = HEAD_DIM ** -0.5 * LOG2E

LANE = 128
MIB = 1024 * 1024

F32 = jnp.float32
BF16 = jnp.bfloat16

A_TQ = 4 * CHUNK
A_WIN = LEFT + A_TQ
A_TU = LEFT + A_WIN
A_ROW = A_TU + A_TQ

X_LANES = 8
X_HEADS = LANE // X_LANES
X_GROUPS = N_HEADS // X_HEADS


def _cparams(n_axes, vmem_mib):
    return pltpu.CompilerParams(dimension_semantics=("arbitrary",) * n_axes,
                                vmem_limit_bytes=vmem_mib * MIB)


def _rmsnorm_kernel(x_ref, g_ref, o_ref):
    x = x_ref[...]
    ms = jnp.mean(x * x, axis=-1, keepdims=True)
    o_ref[...] = (x * lax.rsqrt(ms + EPS) * g_ref[...]).astype(o_ref.dtype)


def _rmsnorm(x, gain, out_dtype, tm=512):
    m, d = x.shape
    return pl.pallas_call(
        _rmsnorm_kernel,
        grid=(m // tm,),
        in_specs=[pl.BlockSpec((tm, d), lambda i: (i, 0)),
                  pl.BlockSpec((1, d), lambda i: (0, 0))],
        out_specs=pl.BlockSpec((tm, d), lambda i: (i, 0)),
        out_shape=jax.ShapeDtypeStruct((m, d), out_dtype),
        compiler_params=_cparams(1, 32),
        name="rmsnorm",
    )(x, gain.reshape(1, d))


def _proj_kernel(h_ref, w_ref, o_ref, *, n_scaled_tiles, scale):
    j = pl.program_id(1)
    acc = jnp.dot(h_ref[...], w_ref[...].astype(BF16), preferred_element_type=F32)
    if n_scaled_tiles:
        acc = acc * jnp.where(j < n_scaled_tiles, scale, 1.0).astype(F32)
    for s in range(o_ref.shape[0]):
        o_ref[s] = acc[:, s * LANE:(s + 1) * LANE].astype(o_ref.dtype)


def _proj(h, w, layer, n_scaled_cols, tm=2048, tn=1024):
    m, k = h.shape
    n = w.shape[2]
    kern = functools.partial(_proj_kernel, n_scaled_tiles=n_scaled_cols // tn, scale=QK_SCALE)
    return pl.pallas_call(
        kern,
        grid=(m // tm, n // tn),
        in_specs=[pl.BlockSpec((tm, k), lambda i, j: (i, 0)),
                  pl.BlockSpec((None, k, tn), lambda i, j: (layer, 0, j))],
        out_specs=pl.BlockSpec((tn // LANE, tm, LANE), lambda i, j: (j, i, 0)),
        out_shape=jax.ShapeDtypeStruct((n // LANE, m, LANE), BF16),
        compiler_params=_cparams(2, 56),
        name="proj",
    )(h, w)


def _emit_skewed(n_tasks, stages):
    k = len(stages)
    order = [0] + list(range(k - 1, 0, -1))
    vals, out = {}, [None] * n_tasks
    for step in range(n_tasks + k - 1):
        for si in order:
            t = step - si
            if 0 <= t < n_tasks:
                res = stages[si](t, vals.pop((t, si - 1)) if si else None)
                if si == k - 1:
                    out[t] = res
                else:
                    vals[(t, si)] = res
    return out


def _bias_kernel(w_ref, o_ref):
    x = jnp.broadcast_to(w_ref[0], (A_TQ, A_ROW))
    r = pltpu.roll(x, A_ROW - A_TQ, 1, stride=1, stride_axis=0)
    i = lax.broadcasted_iota(jnp.int32, (A_TQ, A_TU), 0)
    u = lax.broadcasted_iota(jnp.int32, (A_TQ, A_TU), 1)
    lo = (i // CHUNK) * CHUNK
    valid = (u >= lo) & (u < lo + BAND)
    o_ref[0] = jnp.where(valid, r[:, :A_TU] * LOG2E, NEG_INF).T


def _bias_table(rel_bias):
    h = rel_bias.shape[0]
    n_hi = LEFT + A_TQ - REL_CLIP
    n_lo = A_ROW - n_hi - (2 * REL_CLIP + 1)
    row = jnp.concatenate([jnp.broadcast_to(rel_bias[:, -1:], (h, n_hi)),
                           rel_bias[:, ::-1],
                           jnp.broadcast_to(rel_bias[:, :1], (h, n_lo))], axis=1)
    return pl.pallas_call(
        _bias_kernel,
        grid=(h,),
        in_specs=[pl.BlockSpec((1, 1, A_ROW), lambda i: (i, 0, 0))],
        out_specs=pl.BlockSpec((1, A_TU, A_TQ), lambda i: (i, 0, 0)),
        out_shape=jax.ShapeDtypeStruct((h, A_TU, A_TQ), F32),
        compiler_params=_cparams(1, 32),
        name="bias_table",
    )(row.reshape(h, 1, A_ROW))


def _attn_a_kernel(q_ref, k_ref, v_ref, g_ref, tt_ref, o_ref, *, hb, nt):
    blk = pl.program_id(2)
    tasks = [(hh, t) for t in range(nt) for hh in range(hb)]

    def window(t):
        q0 = (blk * nt + t) * A_TQ
        ws = pl.multiple_of(jnp.maximum(q0 - LEFT, 0), A_TQ)
        u0 = pl.multiple_of(LEFT - jnp.minimum(q0, LEFT), A_TQ)
        return ws, u0

    def scores(i, _):
        hh, t = tasks[i]
        ws, u0 = window(t)
        s = lax.dot_general(k_ref[hh, pl.ds(ws, A_WIN), :], q_ref[hh, t * A_TQ:(t + 1) * A_TQ, :],
                            (((1,), (1,)), ((), ())), preferred_element_type=F32)
        return s + tt_ref[hh, pl.ds(u0, A_WIN), :]

    def softmax(i, s):
        p = jnp.exp2(s - jnp.max(s, axis=0, keepdims=True))
        return p.astype(BF16), jnp.sum(p, axis=0, keepdims=True)

    def output(i, p_l):
        p, l = p_l
        hh, t = tasks[i]
        ws, _ = window(t)
        ot = lax.dot_general(v_ref[hh, pl.ds(ws, A_WIN), :], p, (((0,), (0,)), ((), ())),
                             preferred_element_type=F32)
        g = g_ref[hh, t * A_TQ:(t + 1) * A_TQ, :].astype(F32)
        o_ref[t * A_TQ:(t + 1) * A_TQ, hh * LANE:(hh + 1) * LANE] = (
            (ot / l).T * g * jax.nn.sigmoid(g)).astype(o_ref.dtype)

    _emit_skewed(len(tasks), [scores, softmax, output])


def _attn_a(qkvg, tt, batch, seq, hb=4, nt=8):
    nblk = N_HEADS // hb
    tqb = nt * A_TQ
    nq = seq // tqb
    kern = functools.partial(_attn_a_kernel, hb=hb, nt=nt)

    def slab(role):
        return pl.BlockSpec((hb, seq, LANE), lambda b, h, i: (role * nblk + h, b, 0))

    def tile(role):
        return pl.BlockSpec((hb, tqb, LANE), lambda b, h, i: (role * nblk + h, b * nq + i, 0))

    return pl.pallas_call(
        kern,
        grid=(batch, nblk, nq),
        in_specs=[tile(0), slab(1), slab(2), tile(3),
                  pl.BlockSpec((hb, A_TU, A_TQ), lambda b, h, i: (h, 0, 0))],
        out_specs=pl.BlockSpec((tqb, hb * LANE), lambda b, h, i: (b * nq + i, h)),
        out_shape=jax.ShapeDtypeStruct((batch * seq, D_INNER), BF16),
        compiler_params=_cparams(3, 56),
        name="attn_a",
    )(qkvg, qkvg, qkvg, qkvg, tt)


def _outproj_kernel(a_ref, w_ref, x_ref, o_ref):
    o_ref[...] = x_ref[...] + jnp.dot(a_ref[...], w_ref[...].astype(BF16), preferred_element_type=F32)


def _outproj(a, w, layer, x, tm=1024, tn=512):
    m, k = a.shape
    n = w.shape[2]
    return pl.pallas_call(
        _outproj_kernel,
        grid=(m // tm, n // tn),
        in_specs=[pl.BlockSpec((tm, k), lambda i, j: (i, 0)),
                  pl.BlockSpec((None, k, tn), lambda i, j: (layer, 0, j)),
                  pl.BlockSpec((tm, tn), lambda i, j: (i, j))],
        out_specs=pl.BlockSpec((tm, tn), lambda i, j: (i, j)),
        out_shape=jax.ShapeDtypeStruct((m, n), F32),
        compiler_params=_cparams(2, 56),
        name="outproj",
    )(a, w, x)


def _split3(x):
    hi = x.astype(BF16)
    r1 = x - hi.astype(F32)
    mid = r1.astype(BF16)
    lo = (r1 - mid.astype(F32)).astype(BF16)
    return hi, mid, lo


def _fgate_kernel(h_ref, fw_ref, fb_ref, qx_ref, kx_ref, carry_ref, *, ts):
    @pl.when(pl.program_id(1) == 0)
    def _():
        carry_ref[...] = jnp.zeros_like(carry_ref)

    z = jnp.dot(h_ref[...], fw_ref[...].astype(BF16), preferred_element_type=F32) + fb_ref[...]
    logf = jnp.minimum(z, 0.0) - jnp.log1p(jnp.exp(-jnp.abs(z)))
    r = lax.broadcasted_iota(jnp.int32, (ts, ts), 0)
    c = lax.broadcasted_iota(jnp.int32, (ts, ts), 1)
    tri = jnp.where(r >= c, 1.0, 0.0).astype(BF16)
    cum = carry_ref[...]
    for part in _split3(logf):
        cum = cum + jnp.dot(tri, part, preferred_element_type=F32)
    carry_ref[...] = cum[ts - 1:ts, :]
    parts = jnp.concatenate(_split3(cum * LOG2E), axis=1)
    rr = lax.broadcasted_iota(jnp.int32, (3 * LANE, 2 * LANE), 0)
    cc = lax.broadcasted_iota(jnp.int32, (3 * LANE, 2 * LANE), 1)
    e = cc % X_LANES
    is_q = cc < LANE
    lane = lax.broadcasted_iota(jnp.int32, (1, 2 * LANE), 1)
    le = lane % X_LANES
    ones = jnp.where((lane < LANE) & (le >= 3) & (le < 6) | (lane >= LANE) & (le < 3), 1.0, 0.0)
    for grp in range(X_GROUPS):
        head = grp * X_HEADS + (cc % LANE) // X_LANES
        sel = (jnp.where(is_q & (e < 3) & (rr == e * LANE + head), 1.0, 0.0)
               - jnp.where(~is_q & (e >= 3) & (e < 6) & (rr == (e - 3) * LANE + head), 1.0, 0.0))
        ex = jnp.dot(parts, sel.astype(BF16), preferred_element_type=F32) + ones
        qx_ref[grp] = ex[:, :LANE].astype(qx_ref.dtype)
        kx_ref[grp] = ex[:, LANE:].astype(kx_ref.dtype)


def _fgate(h, f_w, f_b, batch, seq, ts=512):
    m, d = h.shape
    fw = jnp.pad(f_w, ((0, 0), (0, LANE - N_HEADS)))
    fb = jnp.pad(f_b, (0, LANE - N_HEADS)).reshape(1, LANE)
    nblk = seq // ts
    ex_spec = pl.BlockSpec((X_GROUPS, ts, LANE), lambda b, s: (0, b * nblk + s, 0))
    ex_shape = jax.ShapeDtypeStruct((X_GROUPS, m, LANE), BF16)
    return pl.pallas_call(
        functools.partial(_fgate_kernel, ts=ts),
        grid=(batch, nblk),
        in_specs=[pl.BlockSpec((ts, d), lambda b, s: (b * nblk + s, 0)),
                  pl.BlockSpec((d, LANE), lambda b, s: (0, 0)),
                  pl.BlockSpec((1, LANE), lambda b, s: (0, 0))],
        out_specs=(ex_spec, ex_spec),
        out_shape=(ex_shape, ex_shape),
        scratch_shapes=[pltpu.VMEM((1, LANE), F32)],
        compiler_params=_cparams(2, 48),
        name="fgate",
    )(h, fw, fb)


def _attn_b_kernel(q_ref, qx_ref, k_ref, kx_ref, v_ref, g_ref, o_ref, qt_ref, vt_ref, *, hb, tq, tk):
    assert tq == tk
    i = pl.program_id(2)
    row = lax.broadcasted_iota(jnp.int32, (tk, tq), 0)
    col = lax.broadcasted_iota(jnp.int32, (tk, tq), 1)

    lane_head = lax.broadcasted_iota(jnp.int32, (tq, LANE), 1) // X_LANES
    for hh in range(hb):
        own = lane_head == (pl.program_id(1) * hb + hh) % X_HEADS
        qx = jnp.where(own, qx_ref[...], jnp.zeros((), BF16))
        qt_ref[hh] = jnp.concatenate([q_ref[hh], qx], axis=1).T

    @pl.when(i == 0)
    def _():
        for hh in range(hb):
            for c in range(vt_ref.shape[1]):
                vt_ref[hh, c] = v_ref[hh, c * tk:(c + 1) * tk, :].T

    def kv_steps(js, state, diag_off):
        state = list(state)
        tasks = [(j, hh) for j in js for hh in range(hb)]

        def scores(t, _):
            j, hh = tasks[t]
            k0 = pl.multiple_of(j * tk, tk)
            ka = jnp.concatenate([k_ref[hh, pl.ds(k0, tk), :], kx_ref[pl.ds(k0, tk), :]], axis=1)
            s = jnp.dot(ka, qt_ref[hh], preferred_element_type=F32)
            if diag_off is not None:
                s = jnp.where(row + diag_off <= col, s, NEG_INF)
            return s

        def softmax(t, s):
            m, l, _ = state[tasks[t][1]]
            m_new = jnp.maximum(m, jnp.max(s, axis=0, keepdims=True))
            alpha = jnp.exp2(m - m_new)
            p = jnp.exp2(s - m_new)
            return m_new, alpha * l + jnp.sum(p, axis=0, keepdims=True), alpha, p.astype(BF16)

        def accumulate(t, sm):
            j, hh = tasks[t]
            m_new, l_new, alpha, p = sm
            pv = jnp.dot(vt_ref[hh, j], p, preferred_element_type=F32)
            state[hh] = (m_new, l_new, alpha * state[hh][2] + pv)

        _emit_skewed(len(tasks), [scores, lambda t, s: s, softmax, accumulate])
        return tuple(state)

    init = tuple((jnp.full((1, tq), NEG_INF, F32), jnp.zeros((1, tq), F32), jnp.zeros((HEAD_DIM, tq), F32))
                 for _ in range(hb))
    state = lax.fori_loop(0, i // 2, lambda jp, c: kv_steps([2 * jp, 2 * jp + 1], c, None), init)
    state = lax.fori_loop(0, i % 2, lambda _, c: kv_steps([i - 1], c, None), state)
    state = kv_steps([i], state, 0)
    for hh in range(hb):
        _, l, acc = state[hh]
        g = g_ref[hh].astype(F32)
        o_ref[:, hh * LANE:(hh + 1) * LANE] = ((acc / l).T * g * jax.nn.sigmoid(g)).astype(o_ref.dtype)


def _attn_b(qg, kv, qx, kx, batch, seq, hb=4, tq=512, tk=512):
    nblk = N_HEADS // hb
    nq = seq // tq
    kern = functools.partial(_attn_b_kernel, hb=hb, tq=tq, tk=tk)

    def slab(role):
        return pl.BlockSpec((hb, seq, LANE), lambda b, h, i: (role * nblk + h, b, 0))

    def tile(role):
        return pl.BlockSpec((hb, tq, LANE), lambda b, h, i: (role * nblk + h, b * nq + i, 0))

    return pl.pallas_call(
        kern,
        grid=(batch, nblk, nq),
        in_specs=[tile(0), pl.BlockSpec((None, tq, LANE), lambda b, h, i: (h * hb // X_HEADS, b * nq + i, 0)),
                  slab(0), pl.BlockSpec((None, seq, LANE), lambda b, h, i: (h * hb // X_HEADS, b, 0)),
                  slab(1), tile(1)],
        out_specs=pl.BlockSpec((tq, hb * LANE), lambda b, h, i: (b * nq + i, h)),
        out_shape=jax.ShapeDtypeStruct((batch * seq, D_INNER), BF16),
        scratch_shapes=[pltpu.VMEM((hb, 2 * LANE, tq), BF16),
                        pltpu.VMEM((hb, seq // tk, LANE, tk), BF16)],
        compiler_params=_cparams(3, 56),
        name="attn_b",
    )(qg, qx, kv, kx, kv, qg)


def kernel(x, a_norm, a_w_in, a_rel_bias, a_w_out, kv_norm, kv_w, f_w, f_b, b_norm, b_w_in, b_w_out, final_norm):
    batch, seq, d = x.shape
    xr = x.reshape(batch * seq, d)
    for layer in range(a_w_in.shape[0]):
        h = _rmsnorm(xr, a_norm[layer], BF16)
        qkvg = _proj(h, a_w_in, layer, D_INNER)
        tt = _bias_table(a_rel_bias[layer])
        a = _attn_a(qkvg, tt, batch, seq)
        xr = _outproj(a, a_w_out, layer, xr)
    h_kv = _rmsnorm(xr, kv_norm, BF16)
    kv = _proj(h_kv, kv_w[None], 0, 0)
    qx, kx = _fgate(h_kv, f_w, f_b, batch, seq)
    for layer in range(b_w_in.shape[0]):
        h = _rmsnorm(xr, b_norm[layer], BF16)
        qg = _proj(h, b_w_in, layer, D_INNER)
        a = _attn_b(qg, kv, qx, kx, batch, seq)
        xr = _outproj(a, b_w_out, layer, xr)
    return _rmsnorm(xr, final_norm, x.dtype).reshape(batch, seq, d)
```

```python
import functools

import jax
import jax.numpy as jnp
from jax import lax
from jax.experimental import pallas as pl
from jax.experimental.pallas import tpu as pltpu

D_MODEL = 2048
D_INNER = 4096
HEAD_DIM = 128
N_HEADS = D_INNER // HEAD_DIM
CHUNK = 64
N_LEFT_CHUNKS = 8
LEFT = N_LEFT_CHUNKS * CHUNK
BAND = LEFT + CHUNK
REL_CLIP = 256
EPS = 1e-6
NEG_INF = -1e30
LOG2E = 1.4426950408889634
QK_SCALE = HEAD_DIM ** -0.5 * LOG2E

LANE = 128
MIB = 1024 * 1024

F32 = jnp.float32
BF16 = jnp.bfloat16

A_TQ = 4 * CHUNK
A_WIN = LEFT + A_TQ
A_TU = LEFT + A_WIN
A_ROW = A_TU + A_TQ

X_LANES = 8
X_HEADS = LANE // X_LANES
X_GROUPS = N_HEADS // X_HEADS


def _cparams(n_axes, vmem_mib):
    return pltpu.CompilerParams(dimension_semantics=("arbitrary",) * n_axes,
                                vmem_limit_bytes=vmem_mib * MIB)


def _rmsnorm_kernel(x_ref, g_ref, o_ref):
    x = x_ref[...]
    ms = jnp.mean(x * x, axis=-1, keepdims=True)
    o_ref[...] = (x * lax.rsqrt(ms + EPS) * g_ref[...]).astype(o_ref.dtype)


def _rmsnorm(x, gain, out_dtype, tm=512):
    m, d = x.shape
    return pl.pallas_call(
        _rmsnorm_kernel,
        grid=(m // tm,),
        in_specs=[pl.BlockSpec((tm, d), lambda i: (i, 0)),
                  pl.BlockSpec((1, d), lambda i: (0, 0))],
        out_specs=pl.BlockSpec((tm, d), lambda i: (i, 0)),
        out_shape=jax.ShapeDtypeStruct((m, d), out_dtype),
        compiler_params=_cparams(1, 32),
        name="rmsnorm",
    )(x, gain.reshape(1, d))


def _proj_kernel(h_ref, w_ref, *rest, n_scaled_tiles, scale):
    o_ref = rest[-2] if len(rest) == 3 else rest[0]
    j = pl.program_id(1)
    acc = jnp.dot(h_ref[...], w_ref[...].astype(BF16), preferred_element_type=F32)
    if n_scaled_tiles:
        acc = acc * jnp.where(j < n_scaled_tiles, scale, 1.0).astype(F32)
    for s in range(o_ref.shape[0]):
        o_ref[s] = acc[:, s * LANE:(s + 1) * LANE].astype(o_ref.dtype)
    if len(rest) == 3:
        rest[2][...] = rest[0][...].astype(rest[2].dtype)


def _proj(h, w, layer, n_scaled_cols, w2=None, layer2=0, tm=2048, tn=1024):
    m, k = h.shape
    n = w.shape[2]
    nj = n // tn
    kern = functools.partial(_proj_kernel, n_scaled_tiles=n_scaled_cols // tn, scale=QK_SCALE)
    in_specs = [pl.BlockSpec((tm, k), lambda i, j: (i, 0)),
                pl.BlockSpec((None, k, tn), lambda i, j: (layer, 0, j))]
    out_specs = [pl.BlockSpec((tn // LANE, tm, LANE), lambda i, j: (j, i, 0))]
    out_shape = [jax.ShapeDtypeStruct((n // LANE, m, LANE), BF16)]
    operands = [h, w]
    if w2 is not None:
        k2, n2 = w2.shape[1:]
        rows = k2 // ((m // tm) * nj)
        in_specs.append(pl.BlockSpec((None, rows, n2), lambda i, j: (layer2, i * nj + j, 0)))
        out_specs.append(pl.BlockSpec((rows, n2), lambda i, j: (i * nj + j, 0)))
        out_shape.append(jax.ShapeDtypeStruct((k2, n2), BF16))
        operands.append(w2)
    out = pl.pallas_call(
        kern,
        grid=(m // tm, nj),
        in_specs=in_specs,
        out_specs=tuple(out_specs),
        out_shape=tuple(out_shape),
        compiler_params=_cparams(2, 56),
        name="proj",
    )(*operands)
    return out if w2 is not None else out[0]


def _emit_skewed(n_tasks, stages):
    k = len(stages)
    order = [0] + list(range(k - 1, 0, -1))
    vals, out = {}, [None] * n_tasks
    for step in range(n_tasks + k - 1):
        for si in order:
            t = step - si
            if 0 <= t < n_tasks:
                res = stages[si](t, vals.pop((t, si - 1)) if si else None)
                if si == k - 1:
                    out[t] = res
                else:
                    vals[(t, si)] = res
    return out


def _bias_kernel(w_ref, o_ref):
    x = jnp.broadcast_to(w_ref[0], (A_TQ, A_ROW))
    r = pltpu.roll(x, A_ROW - A_TQ, 1, stride=1, stride_axis=0)
    i = lax.broadcasted_iota(jnp.int32, (A_TQ, A_TU), 0)
    u = lax.broadcasted_iota(jnp.int32, (A_TQ, A_TU), 1)
    lo = (i // CHUNK) * CHUNK
    valid = (u >= lo) & (u < lo + BAND)
    o_ref[0] = jnp.where(valid, r[:, :A_TU] * LOG2E, NEG_INF).T


def _bias_table(rel_bias):
    h = rel_bias.shape[0]
    n_hi = LEFT + A_TQ - REL_CLIP
    n_lo = A_ROW - n_hi - (2 * REL_CLIP + 1)
    row = jnp.concatenate([jnp.broadcast_to(rel_bias[:, -1:], (h, n_hi)),
                           rel_bias[:, ::-1],
                           jnp.broadcast_to(rel_bias[:, :1], (h, n_lo))], axis=1)
    return pl.pallas_call(
        _bias_kernel,
        grid=(h,),
        in_specs=[pl.BlockSpec((1, 1, A_ROW), lambda i: (i, 0, 0))],
        out_specs=pl.BlockSpec((1, A_TU, A_TQ), lambda i: (i, 0, 0)),
        out_shape=jax.ShapeDtypeStruct((h, A_TU, A_TQ), F32),
        compiler_params=_cparams(1, 32),
        name="bias_table",
    )(row.reshape(h, 1, A_ROW))


def _attn_a_kernel(q_ref, k_ref, v_ref, g_ref, tt_ref, o_ref, *, hb, nt):
    blk = pl.program_id(2)
    tasks = [(hh, t) for t in range(nt) for hh in range(hb)]

    def window(t):
        q0 = (blk * nt + t) * A_TQ
        ws = pl.multiple_of(jnp.maximum(q0 - LEFT, 0), A_TQ)
        u0 = pl.multiple_of(LEFT - jnp.minimum(q0, LEFT), A_TQ)
        return ws, u0

    def scores(i, _):
        hh, t = tasks[i]
        ws, u0 = window(t)
        s = lax.dot_general(k_ref[hh, pl.ds(ws, A_WIN), :], q_ref[hh, t * A_TQ:(t + 1) * A_TQ, :],
                            (((1,), (1,)), ((), ())), preferred_element_type=F32)
        return s + tt_ref[hh, pl.ds(u0, A_WIN), :]

    def softmax(i, s):
        p = jnp.exp2(s - jnp.max(s, axis=0, keepdims=True))
        return p.astype(BF16), jnp.sum(p, axis=0, keepdims=True)

    def output(i, p_l):
        p, l = p_l
        hh, t = tasks[i]
        ws, _ = window(t)
        ot = lax.dot_general(v_ref[hh, pl.ds(ws, A_WIN), :], p, (((0,), (0,)), ((), ())),
                             preferred_element_type=F32)
        g = g_ref[hh, t * A_TQ:(t + 1) * A_TQ, :].astype(F32)
        o_ref[t * A_TQ:(t + 1) * A_TQ, hh * LANE:(hh + 1) * LANE] = (
            (ot / l).T * g * jax.nn.sigmoid(g)).astype(o_ref.dtype)

    _emit_skewed(len(tasks), [scores, softmax, output])


def _attn_a(qkvg, tt, batch, seq, hb=4, nt=8):
    nblk = N_HEADS // hb
    tqb = nt * A_TQ
    nq = seq // tqb
    kern = functools.partial(_attn_a_kernel, hb=hb, nt=nt)

    def slab(role):
        return pl.BlockSpec((hb, seq, LANE), lambda b, h, i: (role * nblk + h, b, 0))

    def tile(role):
        return pl.BlockSpec((hb, tqb, LANE), lambda b, h, i: (role * nblk + h, b * nq + i, 0))

    return pl.pallas_call(
        kern,
        grid=(batch, nblk, nq),
        in_specs=[tile(0), slab(1), slab(2), tile(3),
                  pl.BlockSpec((hb, A_TU, A_TQ), lambda b, h, i: (h, 0, 0))],
        out_specs=pl.BlockSpec((tqb, hb * LANE), lambda b, h, i: (b * nq + i, h)),
        out_shape=jax.ShapeDtypeStruct((batch * seq, D_INNER), BF16),
        compiler_params=_cparams(3, 56),
        name="attn_a",
    )(qkvg, qkvg, qkvg, qkvg, tt)


def _outproj_kernel(a_ref, w_ref, x_ref, o_ref):
    o_ref[...] = x_ref[...] + jnp.dot(a_ref[...], w_ref[...], preferred_element_type=F32)


def _outproj(a, w, x, tm=1024, tn=512):
    m, k = a.shape
    n = w.shape[1]
    return pl.pallas_call(
        _outproj_kernel,
        grid=(m // tm, n // tn),
        in_specs=[pl.BlockSpec((tm, k), lambda i, j: (i, 0)),
                  pl.BlockSpec((k, tn), lambda i, j: (0, j)),
                  pl.BlockSpec((tm, tn), lambda i, j: (i, j))],
        out_specs=pl.BlockSpec((tm, tn), lambda i, j: (i, j)),
        out_shape=jax.ShapeDtypeStruct((m, n), F32),
        compiler_params=_cparams(2, 56),
        name="outproj",
    )(a, w, x)


def _split3(x):
    hi = x.astype(BF16)
    r1 = x - hi.astype(F32)
    mid = r1.astype(BF16)
    lo = (r1 - mid.astype(F32)).astype(BF16)
    return hi, mid, lo


def _fgate_kernel(h_ref, fw_ref, fb_ref, qx_ref, kx_ref, carry_ref, *, ts):
    @pl.when(pl.program_id(1) == 0)
    def _():
        carry_ref[...] = jnp.zeros_like(carry_ref)

    z = jnp.dot(h_ref[...], fw_ref[...].astype(BF16), preferred_element_type=F32) + fb_ref[...]
    logf = jnp.minimum(z, 0.0) - jnp.log1p(jnp.exp(-jnp.abs(z)))
    r = lax.broadcasted_iota(jnp.int32, (ts, ts), 0)
    c = lax.broadcasted_iota(jnp.int32, (ts, ts), 1)
    tri = jnp.where(r >= c, 1.0, 0.0).astype(BF16)
    cum = carry_ref[...]
    for part in _split3(logf):
        cum = cum + jnp.dot(tri, part, preferred_element_type=F32)
    carry_ref[...] = cum[ts - 1:ts, :]
    parts = jnp.concatenate(_split3(cum * LOG2E), axis=1)
    rr = lax.broadcasted_iota(jnp.int32, (3 * LANE, 2 * LANE), 0)
    cc = lax.broadcasted_iota(jnp.int32, (3 * LANE, 2 * LANE), 1)
    e = cc % X_LANES
    is_q = cc < LANE
    lane = lax.broadcasted_iota(jnp.int32, (1, 2 * LANE), 1)
    le = lane % X_LANES
    ones = jnp.where((lane < LANE) & (le >= 3) & (le < 6) | (lane >= LANE) & (le < 3), 1.0, 0.0)
    for grp in range(X_GROUPS):
        head = grp * X_HEADS + (cc % LANE) // X_LANES
        sel = (jnp.where(is_q & (e < 3) & (rr == e * LANE + head), 1.0, 0.0)
               - jnp.where(~is_q & (e >= 3) & (e < 6) & (rr == (e - 3) * LANE + head), 1.0, 0.0))
        ex = jnp.dot(parts, sel.astype(BF16), preferred_element_type=F32) + ones
        qx_ref[grp] = ex[:, :LANE].astype(qx_ref.dtype)
        kx_ref[grp] = ex[:, LANE:].astype(kx_ref.dtype)


def _fgate(h, f_w, f_b, batch, seq, ts=512):
    m, d = h.shape
    fw = jnp.pad(f_w, ((0, 0), (0, LANE - N_HEADS)))
    fb = jnp.pad(f_b, (0, LANE - N_HEADS)).reshape(1, LANE)
    nblk = seq // ts
    ex_spec = pl.BlockSpec((X_GROUPS, ts, LANE), lambda b, s: (0, b * nblk + s, 0))
    ex_shape = jax.ShapeDtypeStruct((X_GROUPS, m, LANE), BF16)
    return pl.pallas_call(
        functools.partial(_fgate_kernel, ts=ts),
        grid=(batch, nblk),
        in_specs=[pl.BlockSpec((ts, d), lambda b, s: (b * nblk + s, 0)),
                  pl.BlockSpec((d, LANE), lambda b, s: (0, 0)),
                  pl.BlockSpec((1, LANE), lambda b, s: (0, 0))],
        out_specs=(ex_spec, ex_spec),
        out_shape=(ex_shape, ex_shape),
        scratch_shapes=[pltpu.VMEM((1, LANE), F32)],
        compiler_params=_cparams(2, 48),
        name="fgate",
    )(h, fw, fb)


def _attn_b_kernel(q_ref, qx_ref, k_ref, kx_ref, v_ref, g_ref, o_ref, qt_ref, vt_ref, *, hb, tq, tk):
    assert tq == tk
    i = pl.program_id(2)
    row = lax.broadcasted_iota(jnp.int32, (tk, tq), 0)
    col = lax.broadcasted_iota(jnp.int32, (tk, tq), 1)

    lane_head = lax.broadcasted_iota(jnp.int32, (tq, LANE), 1) // X_LANES
    for hh in range(hb):
        own = lane_head == (pl.program_id(1) * hb + hh) % X_HEADS
        qx = jnp.where(own, qx_ref[...], jnp.zeros((), BF16))
        qt_ref[hh] = jnp.concatenate([q_ref[hh], qx], axis=1).T

    @pl.when(i == 0)
    def _():
        for hh in range(hb):
            for c in range(vt_ref.shape[1]):
                vt_ref[hh, c] = v_ref[hh, c * tk:(c + 1) * tk, :].T

    def kv_steps(js, state, diag_off):
        state = list(state)
        tasks = [(j, hh) for j in js for hh in range(hb)]

        def scores(t, _):
            j, hh = tasks[t]
            k0 = pl.multiple_of(j * tk, tk)
            ka = jnp.concatenate([k_ref[hh, pl.ds(k0, tk), :], kx_ref[pl.ds(k0, tk), :]], axis=1)
            s = jnp.dot(ka, qt_ref[hh], preferred_element_type=F32)
            if diag_off is not None:
                s = jnp.where(row + diag_off <= col, s, NEG_INF)
            return s

        def softmax(t, s):
            m, l, _ = state[tasks[t][1]]
            m_new = jnp.maximum(m, jnp.max(s, axis=0, keepdims=True))
            alpha = jnp.exp2(m - m_new)
            p = jnp.exp2(s - m_new)
            return m_new, alpha * l + jnp.sum(p, axis=0, keepdims=True), alpha, p.astype(BF16)

        def accumulate(t, sm):
            j, hh = tasks[t]
            m_new, l_new, alpha, p = sm
            pv = jnp.dot(vt_ref[hh, j], p, preferred_element_type=F32)
            state[hh] = (m_new, l_new, alpha * state[hh][2] + pv)

        _emit_skewed(len(tasks), [scores, lambda t, s: s, softmax, accumulate])
        return tuple(state)

    init = tuple((jnp.full((1, tq), NEG_INF, F32), jnp.zeros((1, tq), F32), jnp.zeros((HEAD_DIM, tq), F32))
                 for _ in range(hb))
    state = lax.fori_loop(0, i // 2, lambda jp, c: kv_steps([2 * jp, 2 * jp + 1], c, None), init)
    state = lax.fori_loop(0, i % 2, lambda _, c: kv_steps([i - 1], c, None), state)
    state = kv_steps([i], state, 0)
    for hh in range(hb):
        _, l, acc = state[hh]
        g = g_ref[hh].astype(F32)
        o_ref[:, hh * LANE:(hh + 1) * LANE] = ((acc / l).T * g * jax.nn.sigmoid(g)).astype(o_ref.dtype)


def _attn_b(qg, kv, qx, kx, batch, seq, hb=4, tq=512, tk=512):
    nblk = N_HEADS // hb
    nq = seq // tq
    kern = functools.partial(_attn_b_kernel, hb=hb, tq=tq, tk=tk)

    def slab(role):
        return pl.BlockSpec((hb, seq, LANE), lambda b, h, i: (role * nblk + h, b, 0))

    def tile(role):
        return pl.BlockSpec((hb, tq, LANE), lambda b, h, i: (role * nblk + h, b * nq + i, 0))

    return pl.pallas_call(
        kern,
        grid=(batch, nblk, nq),
        in_specs=[tile(0), pl.BlockSpec((None, tq, LANE), lambda b, h, i: (h * hb // X_HEADS, b * nq + i, 0)),
                  slab(0), pl.BlockSpec((None, seq, LANE), lambda b, h, i: (h * hb // X_HEADS, b, 0)),
                  slab(1), tile(1)],
        out_specs=pl.BlockSpec((tq, hb * LANE), lambda b, h, i: (b * nq + i, h)),
        out_shape=jax.ShapeDtypeStruct((batch * seq, D_INNER), BF16),
        scratch_shapes=[pltpu.VMEM((hb, 2 * LANE, tq), BF16),
                        pltpu.VMEM((hb, seq // tk, LANE, tk), BF16)],
        compiler_params=_cparams(3, 56),
        name="attn_b",
    )(qg, qx, kv, kx, kv, qg)


def kernel(x, a_norm, a_w_in, a_rel_bias, a_w_out, kv_norm, kv_w, f_w, f_b, b_norm, b_w_in, b_w_out, final_norm):
    batch, seq, d = x.shape
    xr = x.reshape(batch * seq, d)
    for layer in range(a_w_in.shape[0]):
        h = _rmsnorm(xr, a_norm[layer], BF16)
        qkvg, w_out = _proj(h, a_w_in, layer, D_INNER, a_w_out, layer)
        tt = _bias_table(a_rel_bias[layer])
        a = _attn_a(qkvg, tt, batch, seq)
        xr = _outproj(a, w_out, xr)
    h_kv = _rmsnorm(xr, kv_norm, BF16)
    kv = _proj(h_kv, kv_w[None], 0, 0)
    qx, kx = _fgate(h_kv, f_w, f_b, batch, seq)
    for layer in range(b_w_in.shape[0]):
        h = _rmsnorm(xr, b_norm[layer], BF16)
        qg, w_out = _proj(h, b_w_in, layer, D_INNER, b_w_out, layer)
        a = _attn_b(qg, kv, qx, kx, batch, seq)
        xr = _outproj(a, w_out, xr)
    return _rmsnorm(xr, final_norm, x.dtype).reshape(batch, seq, d)
```

```python
import functools

import jax
import jax.numpy as jnp
from jax import lax
from jax.experimental import pallas as pl
from jax.experimental.pallas import tpu as pltpu

D_MODEL = 2048
D_INNER = 4096
HEAD_DIM = 128
N_HEADS = D_INNER // HEAD_DIM
CHUNK = 64
N_LEFT_CHUNKS = 8
LEFT = N_LEFT_CHUNKS * CHUNK
BAND = LEFT + CHUNK
REL_CLIP = 256
EPS = 1e-6
NEG_INF = -1e30
LOG2E = 1.4426950408889634
QK_SCALE = HEAD_DIM ** -0.5 * LOG2E

LANE = 128
MIB = 1024 * 1024

F32 = jnp.float32
BF16 = jnp.bfloat16

A_TQ = 4 * CHUNK
A_WIN = LEFT + A_TQ
A_TU = LEFT + A_WIN
A_ROW = A_TU + A_TQ

X_LANES = 8
X_HEADS = LANE // X_LANES
X_GROUPS = N_HEADS // X_HEADS


def _cparams(n_axes, vmem_mib):
    return pltpu.CompilerParams(dimension_semantics=("arbitrary",) * n_axes,
                                vmem_limit_bytes=vmem_mib * MIB)


def _rmsnorm_kernel(x_ref, g_ref, o_ref):
    x = x_ref[...]
    ms = jnp.mean(x * x, axis=-1, keepdims=True)
    o_ref[...] = (x * lax.rsqrt(ms + EPS) * g_ref[...]).astype(o_ref.dtype)


def _rmsnorm(x, gain, out_dtype, tm=512):
    m, d = x.shape
    return pl.pallas_call(
        _rmsnorm_kernel,
        grid=(m // tm,),
        in_specs=[pl.BlockSpec((tm, d), lambda i: (i, 0)),
                  pl.BlockSpec((1, d), lambda i: (0, 0))],
        out_specs=pl.BlockSpec((tm, d), lambda i: (i, 0)),
        out_shape=jax.ShapeDtypeStruct((m, d), out_dtype),
        compiler_params=_cparams(1, 32),
        name="rmsnorm",
    )(x, gain.reshape(1, d))


def _proj_kernel(h_ref, w_ref, *rest, n_scaled_tiles, scale):
    o_ref = rest[-2] if len(rest) == 3 else rest[0]
    j = pl.program_id(1)
    acc = jnp.dot(h_ref[...], w_ref[...].astype(BF16), preferred_element_type=F32)
    if n_scaled_tiles:
        acc = acc * jnp.where(j < n_scaled_tiles, scale, 1.0).astype(F32)
    for s in range(o_ref.shape[0]):
        o_ref[s] = acc[:, s * LANE:(s + 1) * LANE].astype(o_ref.dtype)
    if len(rest) == 3:
        rest[2][...] = rest[0][...].astype(rest[2].dtype)


def _proj(h, w, layer, n_scaled_cols, w2=None, layer2=0, tm=2048, tn=1024):
    m, k = h.shape
    n = w.shape[2]
    nj = n // tn
    kern = functools.partial(_proj_kernel, n_scaled_tiles=n_scaled_cols // tn, scale=QK_SCALE)
    in_specs = [pl.BlockSpec((tm, k), lambda i, j: (i, 0)),
                pl.BlockSpec((None, k, tn), lambda i, j: (layer, 0, j))]
    out_specs = [pl.BlockSpec((tn // LANE, tm, LANE), lambda i, j: (j, i, 0))]
    out_shape = [jax.ShapeDtypeStruct((n // LANE, m, LANE), BF16)]
    operands = [h, w]
    if w2 is not None:
        k2, n2 = w2.shape[1:]
        rows = k2 // ((m // tm) * nj)
        in_specs.append(pl.BlockSpec((None, rows, n2), lambda i, j: (layer2, i * nj + j, 0)))
        out_specs.append(pl.BlockSpec((rows, n2), lambda i, j: (i * nj + j, 0)))
        out_shape.append(jax.ShapeDtypeStruct((k2, n2), BF16))
        operands.append(w2)
    out = pl.pallas_call(
        kern,
        grid=(m // tm, nj),
        in_specs=in_specs,
        out_specs=tuple(out_specs),
        out_shape=tuple(out_shape),
        compiler_params=_cparams(2, 56),
        name="proj",
    )(*operands)
    return out if w2 is not None else out[0]


def _emit_skewed(n_tasks, stages):
    k = len(stages)
    order = [0] + list(range(k - 1, 0, -1))
    vals, out = {}, [None] * n_tasks
    for step in range(n_tasks + k - 1):
        for si in order:
            t = step - si
            if 0 <= t < n_tasks:
                res = stages[si](t, vals.pop((t, si - 1)) if si else None)
                if si == k - 1:
                    out[t] = res
                else:
                    vals[(t, si)] = res
    return out


def _bias_kernel(w_ref, o_ref):
    x = jnp.broadcast_to(w_ref[0], (A_TQ, A_ROW))
    r = pltpu.roll(x, A_ROW - A_TQ, 1, stride=1, stride_axis=0)
    i = lax.broadcasted_iota(jnp.int32, (A_TQ, A_TU), 0)
    u = lax.broadcasted_iota(jnp.int32, (A_TQ, A_TU), 1)
    lo = (i // CHUNK) * CHUNK
    valid = (u >= lo) & (u < lo + BAND)
    o_ref[0] = jnp.where(valid, r[:, :A_TU] * LOG2E, NEG_INF).T


def _bias_table(rel_bias):
    h = rel_bias.shape[0]
    n_hi = LEFT + A_TQ - REL_CLIP
    n_lo = A_ROW - n_hi - (2 * REL_CLIP + 1)
    row = jnp.concatenate([jnp.broadcast_to(rel_bias[:, -1:], (h, n_hi)),
                           rel_bias[:, ::-1],
                           jnp.broadcast_to(rel_bias[:, :1], (h, n_lo))], axis=1)
    return pl.pallas_call(
        _bias_kernel,
        grid=(h,),
        in_specs=[pl.BlockSpec((1, 1, A_ROW), lambda i: (i, 0, 0))],
        out_specs=pl.BlockSpec((1, A_TU, A_TQ), lambda i: (i, 0, 0)),
        out_shape=jax.ShapeDtypeStruct((h, A_TU, A_TQ), F32),
        compiler_params=_cparams(1, 32),
        name="bias_table",
    )(row.reshape(h, 1, A_ROW))


def _attn_a_kernel(q_ref, k_ref, v_ref, g_ref, tt_ref, o_ref, *, hb, nt):
    blk = pl.program_id(2)
    tasks = [(hh, t) for t in range(nt) for hh in range(hb)]

    def window(t):
        q0 = (blk * nt + t) * A_TQ
        ws = pl.multiple_of(jnp.maximum(q0 - LEFT, 0), A_TQ)
        u0 = pl.multiple_of(LEFT - jnp.minimum(q0, LEFT), A_TQ)
        return ws, u0

    def scores(i, _):
        hh, t = tasks[i]
        ws, u0 = window(t)
        s = lax.dot_general(k_ref[hh, pl.ds(ws, A_WIN), :], q_ref[hh, t * A_TQ:(t + 1) * A_TQ, :],
                            (((1,), (1,)), ((), ())), preferred_element_type=F32)
        return s + tt_ref[hh, pl.ds(u0, A_WIN), :]

    def softmax(i, s):
        p = jnp.exp2(s - jnp.max(s, axis=0, keepdims=True))
        return p.astype(BF16), jnp.sum(p, axis=0, keepdims=True)

    def output(i, p_l):
        p, l = p_l
        hh, t = tasks[i]
        ws, _ = window(t)
        ot = lax.dot_general(v_ref[hh, pl.ds(ws, A_WIN), :], p, (((0,), (0,)), ((), ())),
                             preferred_element_type=F32)
        g = g_ref[hh, t * A_TQ:(t + 1) * A_TQ, :].astype(F32)
        o_ref[t * A_TQ:(t + 1) * A_TQ, hh * LANE:(hh + 1) * LANE] = (
            (ot / l).T * g * jax.nn.sigmoid(g)).astype(o_ref.dtype)

    _emit_skewed(len(tasks), [scores, softmax, output])


def _attn_a(qkvg, tt, batch, seq, hb=4, nt=8):
    nblk = N_HEADS // hb
    tqb = nt * A_TQ
    nq = seq // tqb
    kern = functools.partial(_attn_a_kernel, hb=hb, nt=nt)

    def slab(role):
        return pl.BlockSpec((hb, seq, LANE), lambda b, h, i: (role * nblk + h, b, 0))

    def tile(role):
        return pl.BlockSpec((hb, tqb, LANE), lambda b, h, i: (role * nblk + h, b * nq + i, 0))

    return pl.pallas_call(
        kern,
        grid=(batch, nblk, nq),
        in_specs=[tile(0), slab(1), slab(2), tile(3),
                  pl.BlockSpec((hb, A_TU, A_TQ), lambda b, h, i: (h, 0, 0))],
        out_specs=pl.BlockSpec((tqb, hb * LANE), lambda b, h, i: (b * nq + i, h)),
        out_shape=jax.ShapeDtypeStruct((batch * seq, D_INNER), BF16),
        compiler_params=_cparams(3, 56),
        name="attn_a",
    )(qkvg, qkvg, qkvg, qkvg, tt)


def _outproj_kernel(a_ref, w_ref, x_ref, o_ref):
    o_ref[...] = x_ref[...] + jnp.dot(a_ref[...], w_ref[...], preferred_element_type=F32)


def _outproj(a, w, x, tm=2048, tn=256):
    m, k = a.shape
    n = w.shape[1]
    return pl.pallas_call(
        _outproj_kernel,
        grid=(m // tm, n // tn),
        in_specs=[pl.BlockSpec((tm, k), lambda i, j: (i, 0)),
                  pl.BlockSpec((k, tn), lambda i, j: (0, j)),
                  pl.BlockSpec((tm, tn), lambda i, j: (i, j))],
        out_specs=pl.BlockSpec((tm, tn), lambda i, j: (i, j)),
        out_shape=jax.ShapeDtypeStruct((m, n), F32),
        compiler_params=_cparams(2, 56),
        name="outproj",
    )(a, w, x)


def _split3(x):
    hi = x.astype(BF16)
    r1 = x - hi.astype(F32)
    mid = r1.astype(BF16)
    lo = (r1 - mid.astype(F32)).astype(BF16)
    return hi, mid, lo


def _fgate_kernel(h_ref, fw_ref, fb_ref, qx_ref, kx_ref, carry_ref, *, ts):
    @pl.when(pl.program_id(1) == 0)
    def _():
        carry_ref[...] = jnp.zeros_like(carry_ref)

    z = jnp.dot(h_ref[...], fw_ref[...].astype(BF16), preferred_element_type=F32) + fb_ref[...]
    logf = jnp.minimum(z, 0.0) - jnp.log1p(jnp.exp(-jnp.abs(z)))
    r = lax.broadcasted_iota(jnp.int32, (ts, ts), 0)
    c = lax.broadcasted_iota(jnp.int32, (ts, ts), 1)
    tri = jnp.where(r >= c, 1.0, 0.0).astype(BF16)
    cum = carry_ref[...]
    for part in _split3(logf):
        cum = cum + jnp.dot(tri, part, preferred_element_type=F32)
    carry_ref[...] = cum[ts - 1:ts, :]
    parts = jnp.concatenate(_split3(cum * LOG2E), axis=1)
    rr = lax.broadcasted_iota(jnp.int32, (3 * LANE, 2 * LANE), 0)
    cc = lax.broadcasted_iota(jnp.int32, (3 * LANE, 2 * LANE), 1)
    e = cc % X_LANES
    is_q = cc < LANE
    lane = lax.broadcasted_iota(jnp.int32, (1, 2 * LANE), 1)
    le = lane % X_LANES
    ones = jnp.where((lane < LANE) & (le >= 3) & (le < 6) | (lane >= LANE) & (le < 3), 1.0, 0.0)
    for grp in range(X_GROUPS):
        head = grp * X_HEADS + (cc % LANE) // X_LANES
        sel = (jnp.where(is_q & (e < 3) & (rr == e * LANE + head), 1.0, 0.0)
               - jnp.where(~is_q & (e >= 3) & (e < 6) & (rr == (e - 3) * LANE + head), 1.0, 0.0))
        ex = jnp.dot(parts, sel.astype(BF16), preferred_element_type=F32) + ones
        qx_ref[grp] = ex[:, :LANE].astype(qx_ref.dtype)
        kx_ref[grp] = ex[:, LANE:].astype(kx_ref.dtype)


def _fgate(h, f_w, f_b, batch, seq, ts=512):
    m, d = h.shape
    fw = jnp.pad(f_w, ((0, 0), (0, LANE - N_HEADS)))
    fb = jnp.pad(f_b, (0, LANE - N_HEADS)).reshape(1, LANE)
    nblk = seq // ts
    ex_spec = pl.BlockSpec((X_GROUPS, ts, LANE), lambda b, s: (0, b * nblk + s, 0))
    ex_shape = jax.ShapeDtypeStruct((X_GROUPS, m, LANE), BF16)
    return pl.pallas_call(
        functools.partial(_fgate_kernel, ts=ts),
        grid=(batch, nblk),
        in_specs=[pl.BlockSpec((ts, d), lambda b, s: (b * nblk + s, 0)),
                  pl.BlockSpec((d, LANE), lambda b, s: (0, 0)),
                  pl.BlockSpec((1, LANE), lambda b, s: (0, 0))],
        out_specs=(ex_spec, ex_spec),
        out_shape=(ex_shape, ex_shape),
        scratch_shapes=[pltpu.VMEM((1, LANE), F32)],
        compiler_params=_cparams(2, 48),
        name="fgate",
    )(h, fw, fb)


def _attn_b_kernel(q_ref, qx_ref, k_ref, kx_ref, v_ref, g_ref, o_ref, qt_ref, vt_ref, *, hb, tq, tk):
    assert tq == tk
    i = pl.program_id(2)
    row = lax.broadcasted_iota(jnp.int32, (tk, tq), 0)
    col = lax.broadcasted_iota(jnp.int32, (tk, tq), 1)

    lane_head = lax.broadcasted_iota(jnp.int32, (tq, LANE), 1) // X_LANES
    for hh in range(hb):
        own = lane_head == (pl.program_id(1) * hb + hh) % X_HEADS
        qx = jnp.where(own, qx_ref[...], jnp.zeros((), BF16))
        qt_ref[hh] = jnp.concatenate([q_ref[hh], qx], axis=1).T

    @pl.when(i == 0)
    def _():
        for hh in range(hb):
            for c in range(vt_ref.shape[1]):
                vt_ref[hh, c] = v_ref[hh, c * tk:(c + 1) * tk, :].T

    def kv_steps(js, state, diag_off):
        state = list(state)
        tasks = [(j, hh) for j in js for hh in range(hb)]

        def scores(t, _):
            j, hh = tasks[t]
            k0 = pl.multiple_of(j * tk, tk)
            ka = jnp.concatenate([k_ref[hh, pl.ds(k0, tk), :], kx_ref[pl.ds(k0, tk), :]], axis=1)
            s = jnp.dot(ka, qt_ref[hh], preferred_element_type=F32)
            if diag_off is not None:
                s = jnp.where(row + diag_off <= col, s, NEG_INF)
            return s

        def softmax(t, s):
            m, l, _ = state[tasks[t][1]]
            m_new = jnp.maximum(m, jnp.max(s, axis=0, keepdims=True))
            alpha = jnp.exp2(m - m_new)
            p = jnp.exp2(s - m_new)
            return m_new, alpha * l + jnp.sum(p, axis=0, keepdims=True), alpha, p.astype(BF16)

        def accumulate(t, sm):
            j, hh = tasks[t]
            m_new, l_new, alpha, p = sm
            pv = jnp.dot(vt_ref[hh, j], p, preferred_element_type=F32)
            state[hh] = (m_new, l_new, alpha * state[hh][2] + pv)

        _emit_skewed(len(tasks), [scores, lambda t, s: s, softmax, accumulate])
        return tuple(state)

    init = tuple((jnp.full((1, tq), NEG_INF, F32), jnp.zeros((1, tq), F32), jnp.zeros((HEAD_DIM, tq), F32))
                 for _ in range(hb))
    state = lax.fori_loop(0, i // 2, lambda jp, c: kv_steps([2 * jp, 2 * jp + 1], c, None), init)
    state = lax.fori_loop(0, i % 2, lambda _, c: kv_steps([i - 1], c, None), state)
    state = kv_steps([i], state, 0)
    for hh in range(hb):
        _, l, acc = state[hh]
        g = g_ref[hh].astype(F32)
        o_ref[:, hh * LANE:(hh + 1) * LANE] = ((acc / l).T * g * jax.nn.sigmoid(g)).astype(o_ref.dtype)


def _attn_b(qg, kv, qx, kx, batch, seq, hb=4, tq=512, tk=512):
    nblk = N_HEADS // hb
    nq = seq // tq
    kern = functools.partial(_attn_b_kernel, hb=hb, tq=tq, tk=tk)

    def slab(role):
        return pl.BlockSpec((hb, seq, LANE), lambda b, h, i: (role * nblk + h, b, 0))

    def tile(role):
        return pl.BlockSpec((hb, tq, LANE), lambda b, h, i: (role * nblk + h, b * nq + i, 0))

    return pl.pallas_call(
        kern,
        grid=(batch, nblk, nq),
        in_specs=[tile(0), pl.BlockSpec((None, tq, LANE), lambda b, h, i: (h * hb // X_HEADS, b * nq + i, 0)),
                  slab(0), pl.BlockSpec((None, seq, LANE), lambda b, h, i: (h * hb // X_HEADS, b, 0)),
                  slab(1), tile(1)],
        out_specs=pl.BlockSpec((tq, hb * LANE), lambda b, h, i: (b * nq + i, h)),
        out_shape=jax.ShapeDtypeStruct((batch * seq, D_INNER), BF16),
        scratch_shapes=[pltpu.VMEM((hb, 2 * LANE, tq), BF16),
                        pltpu.VMEM((hb, seq // tk, LANE, tk), BF16)],
        compiler_params=_cparams(3, 56),
        name="attn_b",
    )(qg, qx, kv, kx, kv, qg)


def kernel(x, a_norm, a_w_in, a_rel_bias, a_w_out, kv_norm, kv_w, f_w, f_b, b_norm, b_w_in, b_w_out, final_norm):
    batch, seq, d = x.shape
    xr = x.reshape(batch * seq, d)
    for layer in range(a_w_in.shape[0]):
        h = _rmsnorm(xr, a_norm[layer], BF16)
        qkvg, w_out = _proj(h, a_w_in, layer, D_INNER, a_w_out, layer)
        tt = _bias_table(a_rel_bias[layer])
        a = _attn_a(qkvg, tt, batch, seq)
        xr = _outproj(a, w_out, xr)
    h_kv = _rmsnorm(xr, kv_norm, BF16)
    kv = _proj(h_kv, kv_w[None], 0, 0)
    qx, kx = _fgate(h_kv, f_w, f_b, batch, seq)
    for layer in range(b_w_in.shape[0]):
        h = _rmsnorm(xr, b_norm[layer], BF16)
        qg, w_out = _proj(h, b_w_in, layer, D_INNER, b_w_out, layer)
        a = _attn_b(qg, kv, qx, kx, batch, seq)
        xr = _outproj(a, w_out, xr)
    return _rmsnorm(xr, final_norm, x.dtype).reshape(batch, seq, d)
```
